```python
import math
import jax, jax.numpy as jnp
from jax import lax
import numpy as np

D_MODEL = 1024
BATCH = 4
SEQ = 4096
DEPTH = 4

N_MIXERS = 2
N_ATTN_LAYERS = (DEPTH + 1) // 2
N_POOL_LAYERS = DEPTH // 2
HEAD_DIM = 64
N_HEADS = D_MODEL // HEAD_DIM
N_KV_HEADS = 4
GROUP = N_HEADS // N_KV_HEADS
Q_WIDTH = N_HEADS * HEAD_DIM
KV_WIDTH = N_KV_HEADS * HEAD_DIM
ATTN_GATE_WIDTH = Q_WIDTH
ATTN_IN_WIDTH = Q_WIDTH + 2 * KV_WIDTH + ATTN_GATE_WIDTH
WINDOW = 128
BLOCK = 128
ROT_DIM = HEAD_DIM // 4
ROPE_THETA = 500000.0
POOL_WINDOWS = (2, 4, 8, 16)
N_POOL_GROUPS = len(POOL_WINDOWS)
POOL_WIDTH = D_MODEL
POOL_GROUP_DIM = POOL_WIDTH // N_POOL_GROUPS
POOL_IN_WIDTH = 2 * POOL_WIDTH
NORM_EPS = 1e-6

kernel_name = "hybrid_swa_sink_multiscale_pool_adaln"


def rms_norm(x, g):
    xf = x.astype(jnp.float32)
    y = xf * lax.rsqrt(jnp.mean(xf * xf, axis=-1, keepdims=True) + NORM_EPS)
    return (y * g.astype(jnp.float32)).astype(x.dtype)


def adaln_params(c, w, b):
    mod = jax.nn.silu(c) @ w + b
    shift, scale, gate = jnp.split(mod, 3, axis=-1)
    return shift[:, None, :], scale[:, None, :], gate[:, None, :]


def partial_rope(x, positions):
    half = ROT_DIM // 2
    inv_freq = ROPE_THETA ** (-jnp.arange(half, dtype=jnp.float32) * 2.0 / ROT_DIM)
    ang = positions.astype(jnp.float32)[..., None] * inv_freq
    cos = jnp.cos(ang)[:, :, None, :]
    sin = jnp.sin(ang)[:, :, None, :]
    xf = x.astype(jnp.float32)
    x1, x2 = xf[..., :half], xf[..., half:ROT_DIM]
    rot = jnp.concatenate([x1 * cos - x2 * sin, x2 * cos + x1 * sin], axis=-1)
    return jnp.concatenate([rot.astype(x.dtype), x[..., ROT_DIM:]], axis=-1)


def banded_sink_attention(q, k, v, sinks):
    B, S = q.shape[0], q.shape[1]
    nb = S // BLOCK
    qb = q.reshape(B, nb, BLOCK, N_KV_HEADS, GROUP, HEAD_DIM)

    def band(t):
        tp = jnp.pad(t, ((0, 0), (BLOCK, 0), (0, 0), (0, 0)))
        tb = tp.reshape(B, nb + 1, BLOCK, N_KV_HEADS, HEAD_DIM)
        return jnp.concatenate([tb[:, :-1], tb[:, 1:]], axis=2)

    kb, vb = band(k), band(v)
    scores = jnp.einsum('bnqkgd,bnskd->bnkgqs', qb, kb).astype(jnp.float32) * (HEAD_DIM ** -0.5)
    qi = jnp.arange(BLOCK)[:, None] + BLOCK
    ki = jnp.arange(2 * BLOCK)[None, :]
    diff = qi - ki
    band_ok = (diff >= 0) & (diff < WINDOW)
    key_abs = jnp.arange(nb)[:, None, None] * BLOCK + ki[None] - BLOCK
    mask = band_ok[None] & (key_abs >= 0)
    scores = jnp.where(mask[None, :, None, None], scores, -jnp.inf)
    sink = sinks.astype(jnp.float32).reshape(N_KV_HEADS, GROUP)[None, None, :, :, None, None]
    m = jnp.maximum(jnp.max(scores, axis=-1, keepdims=True), sink)
    p = jnp.exp(scores - m)
    denom = jnp.sum(p, axis=-1, keepdims=True) + jnp.exp(sink - m)
    probs = (p / denom).astype(v.dtype)
    o = jnp.einsum('bnkgqs,bnskd->bnqkgd', probs, vb)
    return o.reshape(B, S, N_HEADS * HEAD_DIM)


def attention_branch(h, positions, w_in, q_norm_g, k_norm_g, sinks, w_out):
    B, S = h.shape[0], h.shape[1]
    proj = h @ w_in
    q, k, v, g = jnp.split(proj, [Q_WIDTH, Q_WIDTH + KV_WIDTH, Q_WIDTH + 2 * KV_WIDTH], axis=-1)
    q = q.reshape(B, S, N_HEADS, HEAD_DIM)
    k = k.reshape(B, S, N_KV_HEADS, HEAD_DIM)
    v = v.reshape(B, S, N_KV_HEADS, HEAD_DIM)
    q = partial_rope(rms_norm(q, q_norm_g), positions)
    k = partial_rope(rms_norm(k, k_norm_g), positions)
    o = banded_sink_attention(q, k, v, sinks)
    return (o * jax.nn.silu(g)) @ w_out


def multiscale_pool(v):
    B, S = v.shape[0], v.shape[1]
    vf = v.astype(jnp.float32).reshape(B, S, N_POOL_GROUPS, POOL_GROUP_DIM)
    cs = jnp.concatenate([jnp.zeros((B, 1, N_POOL_GROUPS, POOL_GROUP_DIM), jnp.float32),
                          jnp.cumsum(vf, axis=1)], axis=1)
    t = jnp.arange(S)
    outs = []
    for gi, w in enumerate(POOL_WINDOWS):
        csg = cs[:, :, gi]
        upper = csg[:, 1:]
        lower = jnp.pad(csg[:, :S - w + 1], ((0, 0), (w - 1, 0), (0, 0)))
        count = jnp.minimum(t + 1, w).astype(jnp.float32)[None, :, None]
        outs.append((upper - lower) / count - vf[:, :, gi])
    return jnp.stack(outs, axis=2).astype(v.dtype)


def pool_branch(h, w_in, w_group, scale, w_out):
    B, S = h.shape[0], h.shape[1]
    proj = h @ w_in
    v, g = jnp.split(proj, [POOL_WIDTH], axis=-1)
    pooled = multiscale_pool(v)
    mixed = jnp.einsum('bsgc,gcd->bsgd', pooled, w_group).reshape(B, S, POOL_WIDTH)
    mixed = mixed * scale
    return (mixed * jax.nn.silu(g)) @ w_out


def setup_inputs(seed: int = 0) -> dict:
    key = jax.random.key(seed)
    ks = jax.random.split(key, 16)
    f32 = jnp.float32
    x = jax.random.normal(ks[0], (BATCH, SEQ, D_MODEL), f32)
    c = jax.random.normal(ks[1], (BATCH, D_MODEL), f32)
    offset = jax.random.randint(ks[2], (BATCH, 1), 0, 1024, dtype=jnp.int32)
    positions = offset + jnp.arange(SEQ, dtype=jnp.int32)[None, :]
    ada_w = jax.random.normal(ks[3], (DEPTH, D_MODEL, 3 * D_MODEL), f32) * (0.5 * D_MODEL ** -0.5)
    ada_b = jax.random.normal(ks[4], (DEPTH, 3 * D_MODEL), f32) * 0.02
    norm_g = 1.0 + 0.02 * jax.random.normal(ks[5], (DEPTH, D_MODEL), f32)
    attn_w_in = jax.random.normal(ks[6], (N_ATTN_LAYERS, D_MODEL, ATTN_IN_WIDTH), f32) * D_MODEL ** -0.5
    attn_q_norm = 1.0 + 0.02 * jax.random.normal(ks[7], (N_ATTN_LAYERS, HEAD_DIM), f32)
    attn_k_norm = 1.0 + 0.02 * jax.random.normal(ks[8], (N_ATTN_LAYERS, HEAD_DIM), f32)
    attn_sinks = 0.5 * jax.random.normal(ks[9], (N_ATTN_LAYERS, N_HEADS), f32)
    attn_w_out = jax.random.normal(ks[10], (N_ATTN_LAYERS, Q_WIDTH, D_MODEL), f32) * Q_WIDTH ** -0.5
    pool_w_in = jax.random.normal(ks[11], (N_POOL_LAYERS, D_MODEL, POOL_IN_WIDTH), f32) * D_MODEL ** -0.5
    pool_w_group = jax.random.normal(ks[12], (N_POOL_LAYERS, N_POOL_GROUPS, POOL_GROUP_DIM, POOL_GROUP_DIM), f32) * POOL_GROUP_DIM ** -0.5
    pool_scale = 1.0 + 0.1 * jax.random.normal(ks[13], (N_POOL_LAYERS, POOL_WIDTH), f32)
    pool_w_out = jax.random.normal(ks[14], (N_POOL_LAYERS, POOL_WIDTH, D_MODEL), f32) * POOL_WIDTH ** -0.5
    return {"x": x, "c": c, "positions": positions, "ada_w": ada_w, "ada_b": ada_b,
            "norm_g": norm_g, "attn_w_in": attn_w_in, "attn_q_norm": attn_q_norm,
            "attn_k_norm": attn_k_norm, "attn_sinks": attn_sinks, "attn_w_out": attn_w_out,
            "pool_w_in": pool_w_in, "pool_w_group": pool_w_group, "pool_scale": pool_scale,
            "pool_w_out": pool_w_out}


def reference(x, c, positions, ada_w, ada_b, norm_g, attn_w_in, attn_q_norm, attn_k_norm,
              attn_sinks, attn_w_out, pool_w_in, pool_w_group, pool_scale, pool_w_out):
    for i in range(DEPTH):
        shift, scale, gate = adaln_params(c, ada_w[i], ada_b[i])
        h = rms_norm(x, norm_g[i]) * (1.0 + scale) + shift
        j = i // N_MIXERS
        if i % N_MIXERS == 0:
            branch = attention_branch(h, positions, attn_w_in[j], attn_q_norm[j], attn_k_norm[j],
                                      attn_sinks[j], attn_w_out[j])
        else:
            branch = pool_branch(h, pool_w_in[j], pool_w_group[j], pool_scale[j], pool_w_out[j])
        x = x + gate * branch
    return x
```

```python
import functools
import math

import jax
import jax.numpy as jnp
from jax import lax
from jax.experimental import pallas as pl
from jax.experimental.pallas import tpu as pltpu

D_MODEL = 1024
HEAD_DIM = 64
N_HEADS = 16
N_KV_HEADS = 4
GROUP = N_HEADS // N_KV_HEADS
Q_WIDTH = N_HEADS * HEAD_DIM
KV_WIDTH = N_KV_HEADS * HEAD_DIM
ATTN_IN_WIDTH = 2 * Q_WIDTH + 2 * KV_WIDTH
BLOCK = 128
ROT_DIM = HEAD_DIM // 4
ROT_HALF = ROT_DIM // 2
ROPE_THETA = 500000.0
POOL_WINDOWS = (2, 4, 8, 16)
POOL_GROUP_DIM = D_MODEL // len(POOL_WINDOWS)
POOL_HALO = 16
NORM_EPS = 1e-6
LANES = 128
LOG2E = math.log2(math.e)

TOKEN_TILE = 256
ROPE_TILE = 512
VMEM_LIMIT = 48 * 1024 * 1024

_NT = (((1,), (1,)), ((), ()))


def _silu(x):
    return x / (1.0 + jnp.exp(-x))


def _adaln_kernel(c_ref, w_ref, b_ref, o_ref):
    a = _silu(c_ref[...]).astype(jnp.bfloat16)
    w = w_ref[0].astype(jnp.bfloat16)
    o_ref[0] = jnp.dot(a, w, preferred_element_type=jnp.float32) + b_ref[0]


def _adaln(c, ada_w, ada_b):
    depth, d, n = ada_w.shape
    b = c.shape[0]
    tn = 1024
    return pl.pallas_call(
        _adaln_kernel,
        grid=(depth, n // tn),
        in_specs=[
            pl.BlockSpec((b, d), lambda l, j: (0, 0)),
            pl.BlockSpec((1, d, tn), lambda l, j: (l, 0, j)),
            pl.BlockSpec((1, 1, tn), lambda l, j: (l, 0, j)),
        ],
        out_specs=pl.BlockSpec((1, b, tn), lambda l, j: (l, 0, j)),
        out_shape=jax.ShapeDtypeStruct((depth, b, n), jnp.float32),
        compiler_params=pltpu.CompilerParams(
            dimension_semantics=("arbitrary", "arbitrary"), vmem_limit_bytes=VMEM_LIMIT),
        name="adaln_mod",
    )(c, ada_w, ada_b.reshape(depth, 1, n))


def _rope_kernel(pos_ref, invf_ref, o_ref):
    ts = pos_ref.shape[-1]
    ang = pos_ref[0].astype(jnp.float32) * invf_ref[...]
    c, s = jnp.cos(ang), jnp.sin(ang)
    one = jnp.ones((ROT_HALF, ts), jnp.float32)
    zero = jnp.zeros((ROT_HALF, ts), jnp.float32)
    per_head = HEAD_DIM // ROT_HALF
    cos_t = jnp.concatenate(([c, c] + [one] * (per_head - 2)) * 2, axis=0)
    sin_lo = jnp.concatenate(([-s] + [zero] * (per_head - 1)) * 2, axis=0)
    sin_hi = jnp.concatenate(([zero, s] + [zero] * (per_head - 2)) * 2, axis=0)
    o_ref[0, :, 0:LANES] = cos_t.T
    o_ref[0, :, LANES:2 * LANES] = sin_lo.T
    o_ref[0, :, 2 * LANES:3 * LANES] = sin_hi.T


def _rope_tables(positions):
    b, s = positions.shape
    inv_freq = ROPE_THETA ** (-jnp.arange(ROT_HALF, dtype=jnp.float32) * 2.0 / ROT_DIM)
    invf = jnp.broadcast_to(inv_freq[:, None], (ROT_HALF, ROPE_TILE))
    return pl.pallas_call(
        _rope_kernel,
        grid=(b, s // ROPE_TILE),
        in_specs=[
            pl.BlockSpec((1, 1, ROPE_TILE), lambda i, j: (i, 0, j)),
            pl.BlockSpec((ROT_HALF, ROPE_TILE), lambda i, j: (0, 0)),
        ],
        out_specs=pl.BlockSpec((1, ROPE_TILE, 3 * LANES), lambda i, j: (i, j, 0)),
        out_shape=jax.ShapeDtypeStruct((b, s, 3 * LANES), jnp.float32),
        compiler_params=pltpu.CompilerParams(
            dimension_semantics=("arbitrary", "arbitrary"), vmem_limit_bytes=VMEM_LIMIT),
        name="rope_tables",
    )(positions.reshape(b, 1, s), invf)


def _modulated_norm(x, mod_ref, g_ref):
    ms = jnp.mean(x * x, axis=-1, keepdims=True)
    y = x * lax.rsqrt(ms + NORM_EPS) * g_ref[...]
    h = y * (1.0 + mod_ref[0, 1:2, :]) + mod_ref[0, 0:1, :]
    return h.astype(jnp.bfloat16)


def _attn_kernel(x_ref, mod_ref, g_ref, rope_ref, w_in_ref, w_out_ref, ind_ref, indt_ref,
                 qg_ref, kg_ref, sink_ref, o_ref,
                 q_s, k_s, v_s, gate_s, og_s):
    tm = x_ref.shape[1]
    j = pl.program_id(1)

    @pl.when(j == 0)
    def _():
        k_s[0:BLOCK, :] = jnp.zeros((BLOCK, k_s.shape[1]), k_s.dtype)
        v_s[0:BLOCK, :] = jnp.zeros((BLOCK, v_s.shape[1]), v_s.dtype)

    x = x_ref[0]
    h = _modulated_norm(x, mod_ref, g_ref)
    proj = jnp.dot(h, w_in_ref[...], preferred_element_type=jnp.float32)
    qk = proj[:, :Q_WIDTH + KV_WIDTH]
    v = proj[:, Q_WIDTH + KV_WIDTH:Q_WIDTH + 2 * KV_WIDTH]
    gate_s[...] = _silu(proj[:, Q_WIDTH + 2 * KV_WIDTH:])

    ssq = jnp.dot((qk * qk).astype(jnp.bfloat16), ind_ref[...],
                  preferred_element_type=jnp.float32)
    r = lax.rsqrt(ssq * (1.0 / HEAD_DIM) + NORM_EPS)
    r_hi = r.astype(jnp.bfloat16)
    r_lo = (r - r_hi.astype(jnp.float32)).astype(jnp.bfloat16)
    rb = jnp.dot(jnp.concatenate([r_hi, r_lo], axis=1), indt_ref[...],
                 preferred_element_type=jnp.float32)

    cos_t = rope_ref[0, :, 0:LANES]
    sin_lo = rope_ref[0, :, LANES:2 * LANES]
    sin_hi = rope_ref[0, :, 2 * LANES:3 * LANES]

    def rope(t):
        return (t * cos_t + pltpu.roll(t, LANES - ROT_HALF, 1) * sin_lo
                + pltpu.roll(t, ROT_HALF, 1) * sin_hi)

    for c in range(Q_WIDTH // LANES):
        sl = slice(c * LANES, (c + 1) * LANES)
        q_s[:, sl] = rope(qk[:, sl] * rb[:, sl] * qg_ref[...]).astype(jnp.bfloat16)

    lane = lax.broadcasted_iota(jnp.int32, (tm, LANES), 1)
    lo_half = lane < HEAD_DIM
    rows = slice(BLOCK, BLOCK + tm)
    for c in range(KV_WIDTH // LANES):
        ksl = slice(Q_WIDTH + c * LANES, Q_WIDTH + (c + 1) * LANES)
        kt = rope(qk[:, ksl] * rb[:, ksl] * kg_ref[...])
        vt = v[:, c * LANES:(c + 1) * LANES]
        for src, dst in ((kt, k_s), (vt, v_s)):
            swapped = pltpu.roll(src, HEAD_DIM, 1)
            for half in range(2):
                kh = 2 * c + half
                nat = jnp.where(lo_half, src, 0.0) if half == 0 else jnp.where(lo_half, 0.0, src)
                oth = jnp.where(lo_half, 0.0, swapped) if half == 0 else jnp.where(lo_half, swapped, 0.0)
                dst[rows, (2 * kh + half) * LANES:(2 * kh + half + 1) * LANES] = nat.astype(dst.dtype)
                dst[rows, (2 * kh + 1 - half) * LANES:(2 * kh + 2 - half) * LANES] = oth.astype(dst.dtype)

    qi = lax.broadcasted_iota(jnp.int32, (BLOCK, 2 * BLOCK), 0) + BLOCK
    ki = lax.broadcasted_iota(jnp.int32, (BLOCK, 2 * BLOCK), 1)
    diff = qi - ki
    band = (diff >= 0) & (diff < BLOCK)
    bias_rest = jnp.where(band, 0.0, -jnp.inf)
    bias_first = jnp.where(band & ((ki >= BLOCK) | (j > 0)), 0.0, -jnp.inf)
    lane_o = lax.broadcasted_iota(jnp.int32, (BLOCK, LANES), 1)

    for qb in range(tm // BLOCK):
        bias = bias_first if qb == 0 else bias_rest
        qrows = slice(qb * BLOCK, (qb + 1) * BLOCK)
        krows = slice(qb * BLOCK, (qb + 2) * BLOCK)
        for kh in range(N_KV_HEADS):
            for pair in range(GROUP // 2):
                tile = kh * (GROUP // 2) + pair
                qt = q_s[qrows, tile * LANES:(tile + 1) * LANES]
                acc = None
                inv = []
                for half in range(2):
                    head = 2 * tile + half
                    kt = k_s[krows, (2 * kh + half) * LANES:(2 * kh + half + 1) * LANES]
                    vt = v_s[krows, (2 * kh + half) * LANES:(2 * kh + half + 1) * LANES]
                    s = lax.dot_general(qt, kt, _NT, preferred_element_type=jnp.float32) + bias
                    sink = sink_ref[head]
                    m = jnp.maximum(jnp.max(s, axis=-1, keepdims=True), sink)
                    p = jnp.exp2(s - m)
                    denom = jnp.sum(p, axis=-1, keepdims=True) + jnp.exp2(sink - m)
                    inv.append(1.0 / denom)
                    o = jnp.dot(p.astype(jnp.bfloat16), vt, preferred_element_type=jnp.float32)
                    acc = o if acc is None else acc + o
                o_pair = acc * jnp.where(lane_o < HEAD_DIM, inv[0], inv[1])
                osl = slice(tile * LANES, (tile + 1) * LANES)
                og_s[qrows, osl] = (o_pair * gate_s[qrows, osl]).astype(jnp.bfloat16)

    out = jnp.dot(og_s[...], w_out_ref[...], preferred_element_type=jnp.float32)
    o_ref[0] = x + mod_ref[0, 2:3, :] * out

    k_s[0:BLOCK, :] = k_s[tm:tm + BLOCK, :]
    v_s[0:BLOCK, :] = v_s[tm:tm + BLOCK, :]


def _attn_layer(x, mod, norm_g, rope, w_in, q_gain, k_gain, sinks, w_out):
    b, s, d = x.shape
    tm = TOKEN_TILE
    n_stat = Q_WIDTH + KV_WIDTH
    head_of_lane = jnp.arange(n_stat) // HEAD_DIM
    ind = (head_of_lane[:, None] == jnp.arange(LANES)[None, :]).astype(jnp.bfloat16)
    indt = jnp.concatenate([ind.T, ind.T], axis=0)
    qg = (jnp.tile(q_gain, LANES // HEAD_DIM) * (HEAD_DIM ** -0.5 * LOG2E)).reshape(1, LANES)
    kg = jnp.tile(k_gain, LANES // HEAD_DIM).reshape(1, LANES)
    const = lambda i, j: (0, 0)
    return pl.pallas_call(
        _attn_kernel,
        grid=(b, s // tm),
        in_specs=[
            pl.BlockSpec((1, tm, d), lambda i, j: (i, j, 0)),
            pl.BlockSpec((1, 3, d), lambda i, j: (i, 0, 0)),
            pl.BlockSpec((1, d), const),
            pl.BlockSpec((1, tm, 3 * LANES), lambda i, j: (i, j, 0)),
            pl.BlockSpec((d, ATTN_IN_WIDTH), const),
            pl.BlockSpec((Q_WIDTH, d), const),
            pl.BlockSpec((n_stat, LANES), const),
            pl.BlockSpec((2 * LANES, n_stat), const),
            pl.BlockSpec((1, LANES), const),
            pl.BlockSpec((1, LANES), const),
            pl.BlockSpec(memory_space=pltpu.SMEM),
        ],
        out_specs=pl.BlockSpec((1, tm, d), lambda i, j: (i, j, 0)),
        out_shape=jax.ShapeDtypeStruct(x.shape, x.dtype),
        scratch_shapes=[
            pltpu.VMEM((tm, Q_WIDTH), jnp.bfloat16),
            pltpu.VMEM((BLOCK + tm, 2 * N_KV_HEADS * LANES), jnp.bfloat16),
            pltpu.VMEM((BLOCK + tm, 2 * N_KV_HEADS * LANES), jnp.bfloat16),
            pltpu.VMEM((tm, Q_WIDTH), jnp.float32),
            pltpu.VMEM((tm, Q_WIDTH), jnp.bfloat16),
        ],
        compiler_params=pltpu.CompilerParams(
            dimension_semantics=("arbitrary", "arbitrary"), vmem_limit_bytes=VMEM_LIMIT),
        name="attn_layer",
    )(x, mod, norm_g.reshape(1, d), rope, w_in.astype(jnp.bfloat16), w_out.astype(jnp.bfloat16),
      ind, indt, qg, kg, sinks * LOG2E)


def _pool_kernel(x_ref, mod_ref, g_ref, w_in_ref, w_grp_ref, scale_ref, w_out_ref, o_ref, v_s):
    tm = x_ref.shape[1]
    j = pl.program_id(1)

    @pl.when(j == 0)
    def _():
        v_s[0:POOL_HALO, :] = jnp.zeros((POOL_HALO, v_s.shape[1]), v_s.dtype)

    x = x_ref[0]
    h = _modulated_norm(x, mod_ref, g_ref)
    proj = jnp.dot(h, w_in_ref[...], preferred_element_type=jnp.float32)
    v_s[POOL_HALO:POOL_HALO + tm, :] = proj[:, :D_MODEL]
    gate = _silu(proj[:, D_MODEL:])

    t = lax.broadcasted_iota(jnp.int32, (tm, POOL_GROUP_DIM), 0) + j * tm
    mixed = []
    for gi, w in enumerate(POOL_WINDOWS):
        cols = slice(gi * POOL_GROUP_DIM, (gi + 1) * POOL_GROUP_DIM)
        cur = v_s[POOL_HALO:POOL_HALO + tm, cols]
        win = cur
        for back in range(1, w):
            win = win + v_s[POOL_HALO - back:POOL_HALO - back + tm, cols]
        count = jnp.minimum(t + 1, w).astype(jnp.float32)
        pooled = win / count - cur
        mixed.append(jnp.dot(pooled.astype(jnp.bfloat16), w_grp_ref[gi],
                             preferred_element_type=jnp.float32))
    mixed = jnp.concatenate(mixed, axis=1) * scale_ref[...]
    out = jnp.dot((mixed * gate).astype(jnp.bfloat16), w_out_ref[...],
                  preferred_element_type=jnp.float32)
    o_ref[0] = x + mod_ref[0, 2:3, :] * out

    v_s[0:POOL_HALO, :] = v_s[tm:tm + POOL_HALO, :]


def _pool_layer(x, mod, norm_g, w_in, w_group, scale, w_out):
    b, s, d = x.shape
    tm = TOKEN_TILE
    const = lambda i, j: (0, 0)
    return pl.pallas_call(
        _pool_kernel,
        grid=(b, s // tm),
        in_specs=[
            pl.BlockSpec((1, tm, d), lambda i, j: (i, j, 0)),
            pl.BlockSpec((1, 3, d), lambda i, j: (i, 0, 0)),
            pl.BlockSpec((1, d), const),
            pl.BlockSpec((d, 2 * D_MODEL), const),
            pl.BlockSpec(w_group.shape, lambda i, j: (0, 0, 0)),
            pl.BlockSpec((1, d), const),
            pl.BlockSpec((D_MODEL, d), const),
        ],
        out_specs=pl.BlockSpec((1, tm, d), lambda i, j: (i, j, 0)),
        out_shape=jax.ShapeDtypeStruct(x.shape, x.dtype),
        scratch_shapes=[pltpu.VMEM((POOL_HALO + tm, D_MODEL), jnp.float32)],
        compiler_params=pltpu.CompilerParams(
            dimension_semantics=("arbitrary", "arbitrary"), vmem_limit_bytes=VMEM_LIMIT),
        name="pool_layer",
    )(x, mod, norm_g.reshape(1, d), w_in.astype(jnp.bfloat16), w_group.astype(jnp.bfloat16),
      scale.reshape(1, d), w_out.astype(jnp.bfloat16))


def kernel(x, c, positions, ada_w, ada_b, norm_g, attn_w_in, attn_q_norm, attn_k_norm, attn_sinks,
           attn_w_out, pool_w_in, pool_w_group, pool_scale, pool_w_out):
    depth = ada_w.shape[0]
    b = x.shape[0]
    mod = _adaln(c, ada_w, ada_b).reshape(depth, b, 3, D_MODEL)
    rope = _rope_tables(positions)
    for i in range(depth):
        jl = i // 2
        if i % 2 == 0:
            x = _attn_layer(x, mod[i], norm_g[i], rope, attn_w_in[jl], attn_q_norm[jl],
                            attn_k_norm[jl], attn_sinks[jl], attn_w_out[jl])
        else:
            x = _pool_layer(x, mod[i], norm_g[i], pool_w_in[jl], pool_w_group[jl],
                            pool_scale[jl], pool_w_out[jl])
    return x
```

```python
import math

import jax
import jax.numpy as jnp
from jax import lax
from jax.experimental import pallas as pl
from jax.experimental.pallas import tpu as pltpu

D_MODEL = 1024
HEAD_DIM = 64
N_HEADS = 16
N_KV_HEADS = 4
GROUP = N_HEADS // N_KV_HEADS
Q_WIDTH = N_HEADS * HEAD_DIM
KV_WIDTH = N_KV_HEADS * HEAD_DIM
ATTN_IN_WIDTH = 2 * Q_WIDTH + 2 * KV_WIDTH
BLOCK = 128
ROT_DIM = HEAD_DIM // 4
ROT_HALF = ROT_DIM // 2
ROPE_THETA = 500000.0
POOL_WINDOWS = (2, 4, 8, 16)
POOL_GROUP_DIM = D_MODEL // len(POOL_WINDOWS)
POOL_HALO = 16
NORM_EPS = 1e-6
LANES = 128
LOG2E = math.log2(math.e)

ATTN_TOKEN_TILE = 512
POOL_TOKEN_TILE = 256
VMEM_LIMIT = 48 * 1024 * 1024

_NT = (((1,), (1,)), ((), ()))
_TN = (((0,), (0,)), ((), ()))


def _silu(x):
    return x / (1.0 + jnp.exp(-x))


def _adaln_kernel(c_ref, w_ref, b_ref, o_ref):
    a = _silu(c_ref[...]).astype(jnp.bfloat16)
    w = w_ref[0].astype(jnp.bfloat16)
    o_ref[0] = jnp.dot(a, w, preferred_element_type=jnp.float32) + b_ref[0]


def _adaln(c, ada_w, ada_b):
    depth, d, n = ada_w.shape
    b = c.shape[0]
    tn = 1024
    return pl.pallas_call(
        _adaln_kernel,
        grid=(depth, n // tn),
        in_specs=[
            pl.BlockSpec((b, d), lambda l, j: (0, 0)),
            pl.BlockSpec((1, d, tn), lambda l, j: (l, 0, j)),
            pl.BlockSpec((1, 1, tn), lambda l, j: (l, 0, j)),
        ],
        out_specs=pl.BlockSpec((1, b, tn), lambda l, j: (l, 0, j)),
        out_shape=jax.ShapeDtypeStruct((depth, b, n), jnp.float32),
        compiler_params=pltpu.CompilerParams(
            dimension_semantics=("arbitrary", "arbitrary"), vmem_limit_bytes=VMEM_LIMIT),
        name="adaln_mod",
    )(c, ada_w, ada_b.reshape(depth, 1, n))


def _modulated_norm(x, mod_ref, g_ref):
    ms = jnp.mean(x * x, axis=-1, keepdims=True)
    y = x * lax.rsqrt(ms + NORM_EPS) * g_ref[...]
    h = y * (1.0 + mod_ref[0, 1:2, :]) + mod_ref[0, 0:1, :]
    return h.astype(jnp.bfloat16)


def _attn_kernel(x_ref, mod_ref, g_ref, pos_ref, invf_ref, w_in_t_ref, w_out_ref,
                 qg_ref, kg_ref, sink_ref, o_ref,
                 qt_s, k_s, vt_s, gt_s, ogt_s):
    tm = x_ref.shape[1]
    j = pl.program_id(1)

    @pl.when(j == 0)
    def _():
        k_s[0:BLOCK, :] = jnp.zeros((BLOCK, k_s.shape[1]), k_s.dtype)
        vt_s[:, 0:BLOCK] = jnp.zeros((vt_s.shape[0], BLOCK), vt_s.dtype)

    x = x_ref[0]
    h = _modulated_norm(x, mod_ref, g_ref)
    projt = lax.dot_general(w_in_t_ref[...], h, _NT, preferred_element_type=jnp.float32)

    ang = pos_ref[0].astype(jnp.float32) * invf_ref[...]
    cos, sin = jnp.cos(ang), jnp.sin(ang)

    def norm_rope(t, gain):
        ssq = jnp.sum(t * t, axis=0, keepdims=True)
        tn = t * lax.rsqrt(ssq * (1.0 / HEAD_DIM) + NORM_EPS) * gain
        x1, x2 = tn[0:ROT_HALF], tn[ROT_HALF:ROT_DIM]
        return jnp.concatenate([x1 * cos - x2 * sin, x2 * cos + x1 * sin, tn[ROT_DIM:]], axis=0)

    for hd in range(N_HEADS):
        rows = slice(hd * HEAD_DIM, (hd + 1) * HEAD_DIM)
        qt_s[rows, :] = norm_rope(projt[rows], qg_ref[...]).astype(jnp.bfloat16)
    for kh in range(N_KV_HEADS):
        rows = slice(Q_WIDTH + kh * HEAD_DIM, Q_WIDTH + (kh + 1) * HEAD_DIM)
        kt = norm_rope(projt[rows], kg_ref[...])
        k_s[BLOCK:BLOCK + tm, kh * LANES:(kh + 1) * LANES] = (
            jnp.concatenate([kt, kt], axis=0).T.astype(jnp.bfloat16))
    vt_s[:, BLOCK:BLOCK + tm] = projt[Q_WIDTH + KV_WIDTH:Q_WIDTH + 2 * KV_WIDTH].astype(jnp.bfloat16)
    gt_s[...] = _silu(projt[Q_WIDTH + 2 * KV_WIDTH:])

    ki = lax.broadcasted_iota(jnp.int32, (2 * BLOCK, BLOCK), 0)
    qi = lax.broadcasted_iota(jnp.int32, (2 * BLOCK, BLOCK), 1) + BLOCK
    diff = qi - ki
    band = (diff >= 0) & (diff < BLOCK)
    bias_rest = jnp.where(band, 0.0, -jnp.inf)
    bias_first = jnp.where(band & ((ki >= BLOCK) | (j > 0)), 0.0, -jnp.inf)
    first_head = lax.broadcasted_iota(jnp.int32, (1, 2 * BLOCK), 1) < BLOCK
    zero = jnp.zeros((HEAD_DIM, BLOCK), jnp.bfloat16)

    n_pairs = N_HEADS // 2

    def scores(qb):
        b1 = bias_first if qb == 0 else bias_rest
        bias = jnp.concatenate([b1, b1], axis=1)
        qcols = slice(qb * BLOCK, (qb + 1) * BLOCK)
        keys = slice(qb * BLOCK, (qb + 2) * BLOCK)
        out = []
        for pr in range(n_pairs):
            kh = pr // (GROUP // 2)
            kd = k_s[keys, kh * LANES:(kh + 1) * LANES]
            r0 = slice(2 * pr * HEAD_DIM, (2 * pr + 1) * HEAD_DIM)
            r1 = slice((2 * pr + 1) * HEAD_DIM, (2 * pr + 2) * HEAD_DIM)
            qz = jnp.concatenate([jnp.concatenate([qt_s[r0, qcols], zero], axis=1),
                                  jnp.concatenate([zero, qt_s[r1, qcols]], axis=1)], axis=0)
            out.append(jnp.dot(kd, qz, preferred_element_type=jnp.float32) + bias)
        return out

    def finish(qb, s_list):
        qcols = slice(qb * BLOCK, (qb + 1) * BLOCK)
        keys = slice(qb * BLOCK, (qb + 2) * BLOCK)
        for pr, s in enumerate(s_list):
            kh = pr // (GROUP // 2)
            vt = vt_s[kh * HEAD_DIM:(kh + 1) * HEAD_DIM, keys]
            r0 = slice(2 * pr * HEAD_DIM, (2 * pr + 1) * HEAD_DIM)
            r1 = slice((2 * pr + 1) * HEAD_DIM, (2 * pr + 2) * HEAD_DIM)
            sink = jnp.where(first_head, sink_ref[2 * pr], sink_ref[2 * pr + 1])
            m = jnp.maximum(jnp.max(s, axis=0, keepdims=True), sink)
            p = jnp.exp2(s - m)
            denom = jnp.sum(p, axis=0, keepdims=True) + jnp.exp2(sink - m)
            o = jnp.dot(vt, p.astype(jnp.bfloat16), preferred_element_type=jnp.float32)
            o = o * (1.0 / denom)
            ogt_s[r0, qcols] = (o[:, :BLOCK] * gt_s[r0, qcols]).astype(jnp.bfloat16)
            ogt_s[r1, qcols] = (o[:, BLOCK:] * gt_s[r1, qcols]).astype(jnp.bfloat16)

    n_qb = tm // BLOCK
    s_next = scores(0)
    for qb in range(n_qb):
        s_cur = s_next
        if qb + 1 < n_qb:
            s_next = scores(qb + 1)
        finish(qb, s_cur)

    out = lax.dot_general(ogt_s[...], w_out_ref[...], _TN, preferred_element_type=jnp.float32)
    o_ref[0] = x + mod_ref[0, 2:3, :] * out

    k_s[0:BLOCK, :] = k_s[tm:tm + BLOCK, :]
    vt_s[:, 0:BLOCK] = vt_s[:, tm:tm + BLOCK]


def _attn_layer(x, mod, norm_g, positions, w_in, q_gain, k_gain, sinks, w_out):
    b, s, d = x.shape
    tm = ATTN_TOKEN_TILE
    inv_freq = ROPE_THETA ** (-jnp.arange(ROT_HALF, dtype=jnp.float32) * 2.0 / ROT_DIM)
    invf = jnp.broadcast_to(inv_freq[:, None], (ROT_HALF, tm))
    qg = jnp.broadcast_to((q_gain * (HEAD_DIM ** -0.5 * LOG2E))[:, None], (HEAD_DIM, tm))
    kg = jnp.broadcast_to(k_gain[:, None], (HEAD_DIM, tm))
    const = lambda i, j: (0, 0)
    return pl.pallas_call(
        _attn_kernel,
        grid=(b, s // tm),
        in_specs=[
            pl.BlockSpec((1, tm, d), lambda i, j: (i, j, 0)),
            pl.BlockSpec((1, 3, d), lambda i, j: (i, 0, 0)),
            pl.BlockSpec((1, d), const),
            pl.BlockSpec((1, 1, tm), lambda i, j: (i, 0, j)),
            pl.BlockSpec((ROT_HALF, tm), const),
            pl.BlockSpec((ATTN_IN_WIDTH, d), const),
            pl.BlockSpec((Q_WIDTH, d), const),
            pl.BlockSpec((HEAD_DIM, tm), const),
            pl.BlockSpec((HEAD_DIM, tm), const),
            pl.BlockSpec(memory_space=pltpu.SMEM),
        ],
        out_specs=pl.BlockSpec((1, tm, d), lambda i, j: (i, j, 0)),
        out_shape=jax.ShapeDtypeStruct(x.shape, x.dtype),
        scratch_shapes=[
            pltpu.VMEM((Q_WIDTH, tm), jnp.bfloat16),
            pltpu.VMEM((BLOCK + tm, N_KV_HEADS * LANES), jnp.bfloat16),
            pltpu.VMEM((KV_WIDTH, BLOCK + tm), jnp.bfloat16),
            pltpu.VMEM((Q_WIDTH, tm), jnp.float32),
            pltpu.VMEM((Q_WIDTH, tm), jnp.bfloat16),
        ],
        compiler_params=pltpu.CompilerParams(
            dimension_semantics=("arbitrary", "arbitrary"), vmem_limit_bytes=VMEM_LIMIT),
        name="attn_layer",
    )(x, mod, norm_g.reshape(1, d), positions.reshape(b, 1, s), invf,
      w_in.T.astype(jnp.bfloat16), w_out.astype(jnp.bfloat16), qg, kg, sinks * LOG2E)


def _pool_kernel(x_ref, mod_ref, g_ref, w_in_ref, w_grp_ref, scale_ref, w_out_ref, o_ref, v_s):
    tm = x_ref.shape[1]
    j = pl.program_id(1)

    @pl.when(j == 0)
    def _():
        v_s[0:POOL_HALO, :] = jnp.zeros((POOL_HALO, v_s.shape[1]), v_s.dtype)

    x = x_ref[0]
    h = _modulated_norm(x, mod_ref, g_ref)
    proj = jnp.dot(h, w_in_ref[...], preferred_element_type=jnp.float32)
    v_s[POOL_HALO:POOL_HALO + tm, :] = proj[:, :D_MODEL]
    gate = _silu(proj[:, D_MODEL:])

    t = lax.broadcasted_iota(jnp.int32, (tm, POOL_GROUP_DIM), 0) + j * tm
    mixed = []
    for gi, w in enumerate(POOL_WINDOWS):
        cols = slice(gi * POOL_GROUP_DIM, (gi + 1) * POOL_GROUP_DIM)
        cur = v_s[POOL_HALO:POOL_HALO + tm, cols]
        win = cur
        for back in range(1, w):
            win = win + v_s[POOL_HALO - back:POOL_HALO - back + tm, cols]
        count = jnp.minimum(t + 1, w).astype(jnp.float32)
        pooled = win / count - cur
        mixed.append(jnp.dot(pooled.astype(jnp.bfloat16), w_grp_ref[gi],
                             preferred_element_type=jnp.float32))
    mixed = jnp.concatenate(mixed, axis=1) * scale_ref[...]
    out = jnp.dot((mixed * gate).astype(jnp.bfloat16), w_out_ref[...],
                  preferred_element_type=jnp.float32)
    o_ref[0] = x + mod_ref[0, 2:3, :] * out

    v_s[0:POOL_HALO, :] = v_s[tm:tm + POOL_HALO, :]


def _pool_layer(x, mod, norm_g, w_in, w_group, scale, w_out):
    b, s, d = x.shape
    tm = POOL_TOKEN_TILE
    const = lambda i, j: (0, 0)
    return pl.pallas_call(
        _pool_kernel,
        grid=(b, s // tm),
        in_specs=[
            pl.BlockSpec((1, tm, d), lambda i, j: (i, j, 0)),
            pl.BlockSpec((1, 3, d), lambda i, j: (i, 0, 0)),
            pl.BlockSpec((1, d), const),
            pl.BlockSpec((d, 2 * D_MODEL), const),
            pl.BlockSpec(w_group.shape, lambda i, j: (0, 0, 0)),
            pl.BlockSpec((1, d), const),
            pl.BlockSpec((D_MODEL, d), const),
        ],
        out_specs=pl.BlockSpec((1, tm, d), lambda i, j: (i, j, 0)),
        out_shape=jax.ShapeDtypeStruct(x.shape, x.dtype),
        scratch_shapes=[pltpu.VMEM((POOL_HALO + tm, D_MODEL), jnp.float32)],
        compiler_params=pltpu.CompilerParams(
            dimension_semantics=("arbitrary", "arbitrary"), vmem_limit_bytes=VMEM_LIMIT),
        name="pool_layer",
    )(x, mod, norm_g.reshape(1, d), w_in.astype(jnp.bfloat16), w_group.astype(jnp.bfloat16),
      scale.reshape(1, d), w_out.astype(jnp.bfloat16))


def kernel(x, c, positions, ada_w, ada_b, norm_g, attn_w_in, attn_q_norm, attn_k_norm, attn_sinks,
           attn_w_out, pool_w_in, pool_w_group, pool_scale, pool_w_out):
    depth = ada_w.shape[0]
    b = x.shape[0]
    mod = _adaln(c, ada_w, ada_b).reshape(depth, b, 3, D_MODEL)
    for i in range(depth):
        jl = i // 2
        if i % 2 == 0:
            x = _attn_layer(x, mod[i], norm_g[i], positions, attn_w_in[jl], attn_q_norm[jl],
                            attn_k_norm[jl], attn_sinks[jl], attn_w_out[jl])
        else:
            x = _pool_layer(x, mod[i], norm_g[i], pool_w_in[jl], pool_w_group[jl],
                            pool_scale[jl], pool_w_out[jl])
    return x
```

```python
import math

import jax
import jax.numpy as jnp
from jax import lax
from jax.experimental import pallas as pl
from jax.experimental.pallas import tpu as pltpu

D_MODEL = 1024
HEAD_DIM = 64
N_HEADS = 16
N_KV_HEADS = 4
GROUP = N_HEADS // N_KV_HEADS
Q_WIDTH = N_HEADS * HEAD_DIM
KV_WIDTH = N_KV_HEADS * HEAD_DIM
ATTN_IN_WIDTH = 2 * Q_WIDTH + 2 * KV_WIDTH
BLOCK = 128
ROT_DIM = HEAD_DIM // 4
ROT_HALF = ROT_DIM // 2
ROPE_THETA = 500000.0
POOL_WINDOWS = (2, 4, 8, 16)
POOL_GROUP_DIM = D_MODEL // len(POOL_WINDOWS)
POOL_HALO = 16
NORM_EPS = 1e-6
LANES = 128
BF16_ROWS = 16
V_PAD = BF16_ROWS
V_ROWS = HEAD_DIM + V_PAD
LOG2E = math.log2(math.e)

ATTN_TOKEN_TILE = 512
POOL_TOKEN_TILE = 512
VMEM_LIMIT = 48 * 1024 * 1024

_NT = (((1,), (1,)), ((), ()))
_TN = (((0,), (0,)), ((), ()))


def _silu(x):
    return x / (1.0 + jnp.exp(-x))


def _adaln_kernel(c_ref, w_ref, b_ref, o_ref):
    a = _silu(c_ref[...]).astype(jnp.bfloat16)
    w = w_ref[0].astype(jnp.bfloat16)
    o_ref[0] = jnp.dot(a, w, preferred_element_type=jnp.float32) + b_ref[0]


def _adaln(c, ada_w, ada_b):
    depth, d, n = ada_w.shape
    b = c.shape[0]
    tn = 1024
    return pl.pallas_call(
        _adaln_kernel,
        grid=(depth, n // tn),
        in_specs=[
            pl.BlockSpec((b, d), lambda l, j: (0, 0)),
            pl.BlockSpec((1, d, tn), lambda l, j: (l, 0, j)),
            pl.BlockSpec((1, 1, tn), lambda l, j: (l, 0, j)),
        ],
        out_specs=pl.BlockSpec((1, b, tn), lambda l, j: (l, 0, j)),
        out_shape=jax.ShapeDtypeStruct((depth, b, n), jnp.float32),
        compiler_params=pltpu.CompilerParams(
            dimension_semantics=("arbitrary", "arbitrary"), vmem_limit_bytes=VMEM_LIMIT),
        name="adaln_mod",
    )(c, ada_w, ada_b.reshape(depth, 1, n))


def _modulated_norm(x, mod_ref, g_ref):
    ms = jnp.mean(x * x, axis=-1, keepdims=True)
    gain = g_ref[...] * (1.0 + mod_ref[0, 1:2, :])
    h = x * lax.rsqrt(ms + NORM_EPS) * gain + mod_ref[0, 0:1, :]
    return h.astype(jnp.bfloat16)


def _attn_kernel(x_ref, mod_ref, g_ref, pos_ref, invf_ref, w_in_t_ref, w_out_ref,
                 qg_ref, kg_ref, sink_ref, o_ref,
                 qt_s, k_s, vt_s, gt_s, ogt_s):
    tm = x_ref.shape[1]
    j = pl.program_id(1)

    @pl.when(j == 0)
    def _():
        k_s[0:BLOCK, :] = jnp.zeros((BLOCK, k_s.shape[1]), k_s.dtype)
        vt_s[:, 0:BLOCK] = jnp.zeros((vt_s.shape[0], BLOCK), vt_s.dtype)

    x = x_ref[0]
    h = _modulated_norm(x, mod_ref, g_ref)
    projt = lax.dot_general(w_in_t_ref[...], h, _NT, preferred_element_type=jnp.float32)

    ang = pos_ref[0].astype(jnp.float32) * invf_ref[...]
    cos, sin = jnp.cos(ang), jnp.sin(ang)

    def norm_rope(t, gain):
        ssq = jnp.sum(t * t, axis=0, keepdims=True)
        tn = t * lax.rsqrt(ssq * (1.0 / HEAD_DIM) + NORM_EPS) * gain
        x1, x2 = tn[0:ROT_HALF], tn[ROT_HALF:ROT_DIM]
        return jnp.concatenate([x1 * cos - x2 * sin, x2 * cos + x1 * sin, tn[ROT_DIM:]], axis=0)

    for hd in range(N_HEADS):
        rows = slice(hd * HEAD_DIM, (hd + 1) * HEAD_DIM)
        qt_s[rows, :] = norm_rope(projt[rows], qg_ref[...]).astype(jnp.bfloat16)
    for kh in range(N_KV_HEADS):
        rows = slice(Q_WIDTH + kh * HEAD_DIM, Q_WIDTH + (kh + 1) * HEAD_DIM)
        kt = norm_rope(projt[rows], kg_ref[...])
        k_s[BLOCK:BLOCK + tm, kh * LANES:(kh + 1) * LANES] = (
            jnp.concatenate([kt, kt], axis=0).T.astype(jnp.bfloat16))
    ones = jnp.ones((V_PAD, tm), jnp.bfloat16)
    for kh in range(N_KV_HEADS):
        vrows = slice(Q_WIDTH + KV_WIDTH + kh * HEAD_DIM, Q_WIDTH + KV_WIDTH + (kh + 1) * HEAD_DIM)
        vt_s[kh * V_ROWS:kh * V_ROWS + HEAD_DIM, BLOCK:BLOCK + tm] = projt[vrows].astype(jnp.bfloat16)
        vt_s[kh * V_ROWS + HEAD_DIM:(kh + 1) * V_ROWS, BLOCK:BLOCK + tm] = ones
    gt_s[...] = _silu(projt[Q_WIDTH + 2 * KV_WIDTH:])

    kk = lax.broadcasted_iota(jnp.int32, (BLOCK, 2 * BLOCK), 0)
    qq = lax.broadcasted_iota(jnp.int32, (BLOCK, 2 * BLOCK), 1) & (BLOCK - 1)
    use_cur = kk <= qq
    prev_bias = jnp.where(j > 0, 0.0, -jnp.inf)
    first_head = lax.broadcasted_iota(jnp.int32, (1, 2 * BLOCK), 1) < BLOCK
    zero = jnp.zeros((HEAD_DIM, BLOCK), jnp.bfloat16)
    zero_p = jnp.zeros((BLOCK, 2 * BLOCK), jnp.bfloat16)

    n_pairs = N_HEADS // 2

    def scores(qb):
        qcols = slice(qb * BLOCK, (qb + 1) * BLOCK)
        keys = slice(qb * BLOCK, (qb + 2) * BLOCK)
        out = []
        for pr in range(n_pairs):
            kh = pr // (GROUP // 2)
            kd = k_s[keys, kh * LANES:(kh + 1) * LANES]
            r0 = slice(2 * pr * HEAD_DIM, (2 * pr + 1) * HEAD_DIM)
            r1 = slice((2 * pr + 1) * HEAD_DIM, (2 * pr + 2) * HEAD_DIM)
            qz = jnp.concatenate([jnp.concatenate([qt_s[r0, qcols], zero], axis=1),
                                  jnp.concatenate([zero, qt_s[r1, qcols]], axis=1)], axis=0)
            s = jnp.dot(kd, qz, preferred_element_type=jnp.float32)
            s_prev = s[:BLOCK] + prev_bias if qb == 0 else s[:BLOCK]
            out.append(jnp.where(use_cur, s[BLOCK:], s_prev))
        return out

    def finish(qb, s_list):
        qcols = slice(qb * BLOCK, (qb + 1) * BLOCK)
        keys = slice(qb * BLOCK, (qb + 2) * BLOCK)
        for pr, s in enumerate(s_list):
            kh = pr // (GROUP // 2)
            vt = vt_s[kh * V_ROWS:(kh + 1) * V_ROWS, keys]
            r0 = slice(2 * pr * HEAD_DIM, (2 * pr + 1) * HEAD_DIM)
            r1 = slice((2 * pr + 1) * HEAD_DIM, (2 * pr + 2) * HEAD_DIM)
            sink = jnp.where(first_head, sink_ref[2 * pr], sink_ref[2 * pr + 1])
            m = jnp.maximum(jnp.max(s, axis=0, keepdims=True), sink)
            p = jnp.exp2(s - m).astype(jnp.bfloat16)
            p2 = jnp.concatenate([jnp.where(use_cur, zero_p, p), jnp.where(use_cur, p, zero_p)], axis=0)
            o = jnp.dot(vt, p2, preferred_element_type=jnp.float32)
            denom = o[HEAD_DIM:HEAD_DIM + 1] + jnp.exp2(sink - m)
            o = o[:HEAD_DIM] * (1.0 / denom)
            ogt_s[r0, qcols] = (o[:, :BLOCK] * gt_s[r0, qcols]).astype(jnp.bfloat16)
            ogt_s[r1, qcols] = (o[:, BLOCK:] * gt_s[r1, qcols]).astype(jnp.bfloat16)

    n_qb = tm // BLOCK
    s_next = scores(0)
    for qb in range(n_qb):
        s_cur = s_next
        if qb + 1 < n_qb:
            s_next = scores(qb + 1)
        finish(qb, s_cur)

    out = lax.dot_general(ogt_s[...], w_out_ref[...], _TN, preferred_element_type=jnp.float32)
    o_ref[0] = x + mod_ref[0, 2:3, :] * out

    k_s[0:BLOCK, :] = k_s[tm:tm + BLOCK, :]
    vt_s[:, 0:BLOCK] = vt_s[:, tm:tm + BLOCK]


def _attn_layer(x, mod, norm_g, positions, w_in, q_gain, k_gain, sinks, w_out):
    b, s, d = x.shape
    tm = ATTN_TOKEN_TILE
    inv_freq = ROPE_THETA ** (-jnp.arange(ROT_HALF, dtype=jnp.float32) * 2.0 / ROT_DIM)
    invf = jnp.broadcast_to(inv_freq[:, None], (ROT_HALF, tm))
    qg = jnp.broadcast_to((q_gain * (HEAD_DIM ** -0.5 * LOG2E))[:, None], (HEAD_DIM, tm))
    kg = jnp.broadcast_to(k_gain[:, None], (HEAD_DIM, tm))
    const = lambda i, j: (0, 0)
    return pl.pallas_call(
        _attn_kernel,
        grid=(b, s // tm),
        in_specs=[
            pl.BlockSpec((1, tm, d), lambda i, j: (i, j, 0)),
            pl.BlockSpec((1, 3, d), lambda i, j: (i, 0, 0)),
            pl.BlockSpec((1, d), const),
            pl.BlockSpec((1, 1, tm), lambda i, j: (i, 0, j)),
            pl.BlockSpec((ROT_HALF, tm), const),
            pl.BlockSpec((ATTN_IN_WIDTH, d), const),
            pl.BlockSpec((Q_WIDTH, d), const),
            pl.BlockSpec((HEAD_DIM, tm), const),
            pl.BlockSpec((HEAD_DIM, tm), const),
            pl.BlockSpec(memory_space=pltpu.SMEM),
        ],
        out_specs=pl.BlockSpec((1, tm, d), lambda i, j: (i, j, 0)),
        out_shape=jax.ShapeDtypeStruct(x.shape, x.dtype),
        scratch_shapes=[
            pltpu.VMEM((Q_WIDTH, tm), jnp.bfloat16),
            pltpu.VMEM((BLOCK + tm, N_KV_HEADS * LANES), jnp.bfloat16),
            pltpu.VMEM((N_KV_HEADS * V_ROWS, BLOCK + tm), jnp.bfloat16),
            pltpu.VMEM((Q_WIDTH, tm), jnp.float32),
            pltpu.VMEM((Q_WIDTH, tm), jnp.bfloat16),
        ],
        compiler_params=pltpu.CompilerParams(
            dimension_semantics=("arbitrary", "arbitrary"), vmem_limit_bytes=VMEM_LIMIT),
        name="attn_layer",
    )(x, mod, norm_g.reshape(1, d), positions.reshape(b, 1, s), invf,
      w_in.T.astype(jnp.bfloat16), w_out.astype(jnp.bfloat16), qg, kg, sinks * LOG2E)


def _pool_kernel(x_ref, mod_ref, g_ref, w_in_ref, w_grp_ref, scale_ref, w_out_ref, o_ref, v_s):
    tm = x_ref.shape[1]
    j = pl.program_id(1)

    @pl.when(j == 0)
    def _():
        v_s[0:POOL_HALO, :] = jnp.zeros((POOL_HALO, v_s.shape[1]), v_s.dtype)

    x = x_ref[0]
    h = _modulated_norm(x, mod_ref, g_ref)
    proj = jnp.dot(h, w_in_ref[...], preferred_element_type=jnp.float32)
    v_s[POOL_HALO:POOL_HALO + tm, :] = proj[:, :D_MODEL]
    gate = _silu(proj[:, D_MODEL:])

    t = lax.broadcasted_iota(jnp.int32, (tm, POOL_GROUP_DIM), 0) + j * tm
    mixed = []
    for gi, w in enumerate(POOL_WINDOWS):
        cols = slice(gi * POOL_GROUP_DIM, (gi + 1) * POOL_GROUP_DIM)
        cur = v_s[POOL_HALO:POOL_HALO + tm, cols]
        win = v_s[:, cols]
        span = 1
        while span < w:
            win = win + pltpu.roll(win, span, 0)
            span *= 2
        win = win[POOL_HALO:]
        count = jnp.minimum(t + 1, w).astype(jnp.float32)
        pooled = win / count - cur
        mixed.append(jnp.dot(pooled.astype(jnp.bfloat16), w_grp_ref[gi],
                             preferred_element_type=jnp.float32))
    mixed = jnp.concatenate(mixed, axis=1) * scale_ref[...]
    out = jnp.dot((mixed * gate).astype(jnp.bfloat16), w_out_ref[...],
                  preferred_element_type=jnp.float32)
    o_ref[0] = x + mod_ref[0, 2:3, :] * out

    v_s[0:POOL_HALO, :] = v_s[tm:tm + POOL_HALO, :]


def _pool_layer(x, mod, norm_g, w_in, w_group, scale, w_out):
    b, s, d = x.shape
    tm = POOL_TOKEN_TILE
    const = lambda i, j: (0, 0)
    return pl.pallas_call(
        _pool_kernel,
        grid=(b, s // tm),
        in_specs=[
            pl.BlockSpec((1, tm, d), lambda i, j: (i, j, 0)),
            pl.BlockSpec((1, 3, d), lambda i, j: (i, 0, 0)),
            pl.BlockSpec((1, d), const),
            pl.BlockSpec((d, 2 * D_MODEL), const),
            pl.BlockSpec(w_group.shape, lambda i, j: (0, 0, 0)),
            pl.BlockSpec((1, d), const),
            pl.BlockSpec((D_MODEL, d), const),
        ],
        out_specs=pl.BlockSpec((1, tm, d), lambda i, j: (i, j, 0)),
        out_shape=jax.ShapeDtypeStruct(x.shape, x.dtype),
        scratch_shapes=[pltpu.VMEM((POOL_HALO + tm, D_MODEL), jnp.float32)],
        compiler_params=pltpu.CompilerParams(
            dimension_semantics=("arbitrary", "arbitrary"), vmem_limit_bytes=VMEM_LIMIT),
        name="pool_layer",
    )(x, mod, norm_g.reshape(1, d), w_in.astype(jnp.bfloat16), w_group.astype(jnp.bfloat16),
      scale.reshape(1, d), w_out.astype(jnp.bfloat16))


def kernel(x, c, positions, ada_w, ada_b, norm_g, attn_w_in, attn_q_norm, attn_k_norm, attn_sinks,
           attn_w_out, pool_w_in, pool_w_group, pool_scale, pool_w_out):
    depth = ada_w.shape[0]
    b = x.shape[0]
    mod = _adaln(c, ada_w, ada_b).reshape(depth, b, 3, D_MODEL)
    for i in range(depth):
        jl = i // 2
        if i % 2 == 0:
            x = _attn_layer(x, mod[i], norm_g[i], positions, attn_w_in[jl], attn_q_norm[jl],
                            attn_k_norm[jl], attn_sinks[jl], attn_w_out[jl])
        else:
            x = _pool_layer(x, mod[i], norm_g[i], pool_w_in[jl], pool_w_group[jl],
                            pool_scale[jl], pool_w_out[jl])
    return x
```

```python
import math

import jax
import jax.numpy as jnp
from jax import lax
from jax.experimental import pallas as pl
from jax.experimental.pallas import tpu as pltpu

D_MODEL = 1024
HEAD_DIM = 64
N_HEADS = 16
N_KV_HEADS = 4
GROUP = N_HEADS // N_KV_HEADS
Q_WIDTH = N_HEADS * HEAD_DIM
KV_WIDTH = N_KV_HEADS * HEAD_DIM
ATTN_IN_WIDTH = 2 * Q_WIDTH + 2 * KV_WIDTH
BLOCK = 128
ROT_DIM = HEAD_DIM // 4
ROT_HALF = ROT_DIM // 2
ROPE_THETA = 500000.0
POOL_WINDOWS = (2, 4, 8, 16)
POOL_GROUP_DIM = D_MODEL // len(POOL_WINDOWS)
POOL_HALO = 16
NORM_EPS = 1e-6
LANES = 128
BF16_ROWS = 16
V_PAD = BF16_ROWS
V_ROWS = HEAD_DIM + V_PAD
LOG2E = math.log2(math.e)

ATTN_HALF_TILE = 512
PROJ_CHUNK = 256
POOL_TOKEN_TILE = 512
VMEM_LIMIT = 56 * 1024 * 1024

_NT = (((1,), (1,)), ((), ()))
_TN = (((0,), (0,)), ((), ()))


def _silu(x):
    return x / (1.0 + jnp.exp(-x))


def _adaln_kernel(c_ref, w_ref, b_ref, o_ref):
    a = _silu(c_ref[...]).astype(jnp.bfloat16)
    w = w_ref[0].astype(jnp.bfloat16)
    o_ref[0] = jnp.dot(a, w, preferred_element_type=jnp.float32) + b_ref[0]


def _adaln(c, ada_w, ada_b):
    depth, d, n = ada_w.shape
    b = c.shape[0]
    tn = 1024
    return pl.pallas_call(
        _adaln_kernel,
        grid=(depth, n // tn),
        in_specs=[
            pl.BlockSpec((b, d), lambda l, j: (0, 0)),
            pl.BlockSpec((1, d, tn), lambda l, j: (l, 0, j)),
            pl.BlockSpec((1, 1, tn), lambda l, j: (l, 0, j)),
        ],
        out_specs=pl.BlockSpec((1, b, tn), lambda l, j: (l, 0, j)),
        out_shape=jax.ShapeDtypeStruct((depth, b, n), jnp.float32),
        compiler_params=pltpu.CompilerParams(
            dimension_semantics=("arbitrary", "arbitrary"), vmem_limit_bytes=VMEM_LIMIT),
        name="adaln_mod",
    )(c, ada_w, ada_b.reshape(depth, 1, n))


def _modulated_norm(x, mod_ref, g_ref):
    ms = jnp.mean(x * x, axis=-1, keepdims=True)
    gain = g_ref[...] * (1.0 + mod_ref[0, 1:2, :])
    h = x * lax.rsqrt(ms + NORM_EPS) * gain + mod_ref[0, 0:1, :]
    return h.astype(jnp.bfloat16)


def _attn_kernel(x_ref, mod_ref, g_ref, pos_ref, invf_ref, w_in_t_ref, w_out_ref,
                 qg_ref, kg_ref, sink_ref, o_ref,
                 qt_s, k_s, vt_s, gt_s, ogt_s):
    tm = ATTN_HALF_TILE
    j = pl.program_id(1)

    @pl.when(j == 0)
    def _():
        k_s[0:BLOCK, :] = jnp.zeros((BLOCK, k_s.shape[1]), k_s.dtype)
        vt_s[:, 0:BLOCK] = jnp.zeros((vt_s.shape[0], BLOCK), vt_s.dtype)

    kk = lax.broadcasted_iota(jnp.int32, (BLOCK, 2 * BLOCK), 0)
    qq = lax.broadcasted_iota(jnp.int32, (BLOCK, 2 * BLOCK), 1) & (BLOCK - 1)
    use_cur = kk <= qq
    prev_bias = jnp.where(j > 0, 0.0, -jnp.inf)
    first_head = lax.broadcasted_iota(jnp.int32, (1, 2 * BLOCK), 1) < BLOCK
    zero = jnp.zeros((HEAD_DIM, BLOCK), jnp.bfloat16)
    zero_p = jnp.zeros((BLOCK, 2 * BLOCK), jnp.bfloat16)
    ones = jnp.ones((V_PAD, tm), jnp.bfloat16)
    n_pairs = N_HEADS // 2

    def projection(hf):
        toks = slice(hf * tm, (hf + 1) * tm)
        krows = slice(BLOCK + hf * tm, BLOCK + (hf + 1) * tm)
        st = {}

        def norm():
            st["h"] = _modulated_norm(x_ref[0, toks, :], mod_ref, g_ref)
            ang = pos_ref[0, :, toks].astype(jnp.float32) * invf_ref[...]
            st["cos"], st["sin"] = jnp.cos(ang), jnp.sin(ang)

        def norm_rope(t, gain):
            ssq = jnp.sum(t * t, axis=0, keepdims=True)
            tn = t * lax.rsqrt(ssq * (1.0 / HEAD_DIM) + NORM_EPS) * gain
            x1, x2 = tn[0:ROT_HALF], tn[ROT_HALF:ROT_DIM]
            cos, sin = st["cos"], st["sin"]
            return jnp.concatenate([x1 * cos - x2 * sin, x2 * cos + x1 * sin, tn[ROT_DIM:]], axis=0)

        def chunk(c):
            rows = slice(c * PROJ_CHUNK, (c + 1) * PROJ_CHUNK)
            pt = lax.dot_general(w_in_t_ref[rows, :], st["h"], _NT,
                                 preferred_element_type=jnp.float32)
            heads = [pt[i * HEAD_DIM:(i + 1) * HEAD_DIM] for i in range(PROJ_CHUNK // HEAD_DIM)]
            if c < Q_WIDTH // PROJ_CHUNK:
                for i, t in enumerate(heads):
                    hd = c * (PROJ_CHUNK // HEAD_DIM) + i
                    qt_s[hd * HEAD_DIM:(hd + 1) * HEAD_DIM, toks] = (
                        norm_rope(t, qg_ref[...]).astype(jnp.bfloat16))
            elif c == Q_WIDTH // PROJ_CHUNK:
                for kh, t in enumerate(heads):
                    kt = norm_rope(t, kg_ref[...])
                    k_s[krows, kh * LANES:(kh + 1) * LANES] = (
                        jnp.concatenate([kt, kt], axis=0).T.astype(jnp.bfloat16))
            elif c == Q_WIDTH // PROJ_CHUNK + 1:
                for kh, t in enumerate(heads):
                    vt_s[kh * V_ROWS:kh * V_ROWS + HEAD_DIM, krows] = t.astype(jnp.bfloat16)
                    vt_s[kh * V_ROWS + HEAD_DIM:(kh + 1) * V_ROWS, krows] = ones
            else:
                g0 = (c - Q_WIDTH // PROJ_CHUNK - 2) * PROJ_CHUNK
                gt_s[g0:g0 + PROJ_CHUNK, toks] = _silu(pt)

        return [norm] + [(lambda c=c: chunk(c)) for c in range(ATTN_IN_WIDTH // PROJ_CHUNK)]

    def scores(qb):
        qcols = slice(qb * BLOCK, (qb + 1) * BLOCK)
        keys = slice(qb * BLOCK, (qb + 2) * BLOCK)
        out = []
        for pr in range(n_pairs):
            kh = pr // (GROUP // 2)
            kd = k_s[keys, kh * LANES:(kh + 1) * LANES]
            r0 = slice(2 * pr * HEAD_DIM, (2 * pr + 1) * HEAD_DIM)
            r1 = slice((2 * pr + 1) * HEAD_DIM, (2 * pr + 2) * HEAD_DIM)
            qz = jnp.concatenate([jnp.concatenate([qt_s[r0, qcols], zero], axis=1),
                                  jnp.concatenate([zero, qt_s[r1, qcols]], axis=1)], axis=0)
            s = jnp.dot(kd, qz, preferred_element_type=jnp.float32)
            s_prev = s[:BLOCK] + prev_bias if qb == 0 else s[:BLOCK]
            out.append(jnp.where(use_cur, s[BLOCK:], s_prev))
        return out

    def finish(qb, s_list):
        qcols = slice(qb * BLOCK, (qb + 1) * BLOCK)
        keys = slice(qb * BLOCK, (qb + 2) * BLOCK)
        for pr, s in enumerate(s_list):
            kh = pr // (GROUP // 2)
            vt = vt_s[kh * V_ROWS:(kh + 1) * V_ROWS, keys]
            r0 = slice(2 * pr * HEAD_DIM, (2 * pr + 1) * HEAD_DIM)
            r1 = slice((2 * pr + 1) * HEAD_DIM, (2 * pr + 2) * HEAD_DIM)
            sink = jnp.where(first_head, sink_ref[2 * pr], sink_ref[2 * pr + 1])
            m = jnp.maximum(jnp.max(s, axis=0, keepdims=True), sink)
            p = jnp.exp2(s - m).astype(jnp.bfloat16)
            p2 = jnp.concatenate([jnp.where(use_cur, zero_p, p), jnp.where(use_cur, p, zero_p)], axis=0)
            o = jnp.dot(vt, p2, preferred_element_type=jnp.float32)
            denom = o[HEAD_DIM:HEAD_DIM + 1] + jnp.exp2(sink - m)
            o = o[:HEAD_DIM] * (1.0 / denom)
            ogt_s[r0, qcols] = (o[:, :BLOCK] * gt_s[r0, qcols]).astype(jnp.bfloat16)
            ogt_s[r1, qcols] = (o[:, BLOCK:] * gt_s[r1, qcols]).astype(jnp.bfloat16)

    def attention(hf):
        blocks = range(hf * (tm // BLOCK), (hf + 1) * (tm // BLOCK))
        st = {}

        def first():
            st["s"] = scores(blocks[0])

        def step(qb):
            s_cur = st["s"]
            if qb + 1 in blocks:
                st["s"] = scores(qb + 1)
            finish(qb, s_cur)

        return [first] + [(lambda qb=qb: step(qb)) for qb in blocks]

    def out_projection(hf):
        toks = slice(hf * tm, (hf + 1) * tm)

        def run():
            out = lax.dot_general(ogt_s[:, toks], w_out_ref[...], _TN,
                                  preferred_element_type=jnp.float32)
            o_ref[0, toks, :] = x_ref[0, toks, :] + mod_ref[0, 2:3, :] * out

        return [run]

    def interleave(a, b):
        done_a = done_b = 0
        while done_a < len(a) or done_b < len(b):
            if done_b >= len(b) or (done_a < len(a) and done_a * len(b) <= done_b * len(a)):
                a[done_a]()
                done_a += 1
            else:
                b[done_b]()
                done_b += 1

    proj0, proj1 = projection(0), projection(1)
    for f in proj0:
        f()
    proj1[0]()
    interleave(attention(0), proj1[1:])
    interleave(attention(1), out_projection(0))
    out_projection(1)[0]()

    step_tokens = 2 * tm
    k_s[0:BLOCK, :] = k_s[step_tokens:step_tokens + BLOCK, :]
    vt_s[:, 0:BLOCK] = vt_s[:, step_tokens:step_tokens + BLOCK]


def _attn_layer(x, mod, norm_g, positions, w_in, q_gain, k_gain, sinks, w_out):
    b, s, d = x.shape
    tm = ATTN_HALF_TILE
    step = 2 * tm
    inv_freq = ROPE_THETA ** (-jnp.arange(ROT_HALF, dtype=jnp.float32) * 2.0 / ROT_DIM)
    invf = jnp.broadcast_to(inv_freq[:, None], (ROT_HALF, tm))
    qg = jnp.broadcast_to((q_gain * (HEAD_DIM ** -0.5 * LOG2E))[:, None], (HEAD_DIM, tm))
    kg = jnp.broadcast_to(k_gain[:, None], (HEAD_DIM, tm))
    const = lambda i, j: (0, 0)
    return pl.pallas_call(
        _attn_kernel,
        grid=(b, s // step),
        in_specs=[
            pl.BlockSpec((1, step, d), lambda i, j: (i, j, 0)),
            pl.BlockSpec((1, 3, d), lambda i, j: (i, 0, 0)),
            pl.BlockSpec((1, d), const),
            pl.BlockSpec((1, 1, step), lambda i, j: (i, 0, j)),
            pl.BlockSpec((ROT_HALF, tm), const),
            pl.BlockSpec((ATTN_IN_WIDTH, d), const),
            pl.BlockSpec((Q_WIDTH, d), const),
            pl.BlockSpec((HEAD_DIM, tm), const),
            pl.BlockSpec((HEAD_DIM, tm), const),
            pl.BlockSpec(memory_space=pltpu.SMEM),
        ],
        out_specs=pl.BlockSpec((1, step, d), lambda i, j: (i, j, 0)),
        out_shape=jax.ShapeDtypeStruct(x.shape, x.dtype),
        scratch_shapes=[
            pltpu.VMEM((Q_WIDTH, step), jnp.bfloat16),
            pltpu.VMEM((BLOCK + step, N_KV_HEADS * LANES), jnp.bfloat16),
            pltpu.VMEM((N_KV_HEADS * V_ROWS, BLOCK + step), jnp.bfloat16),
            pltpu.VMEM((Q_WIDTH, step), jnp.float32),
            pltpu.VMEM((Q_WIDTH, step), jnp.bfloat16),
        ],
        compiler_params=pltpu.CompilerParams(
            dimension_semantics=("arbitrary", "arbitrary"), vmem_limit_bytes=VMEM_LIMIT),
        name="attn_layer",
    )(x, mod, norm_g.reshape(1, d), positions.reshape(b, 1, s), invf,
      w_in.T.astype(jnp.bfloat16), w_out.astype(jnp.bfloat16), qg, kg, sinks * LOG2E)


def _pool_kernel(x_ref, mod_ref, g_ref, w_in_ref, w_grp_ref, scale_ref, w_out_ref, o_ref, v_s):
    tm = x_ref.shape[1]
    j = pl.program_id(1)

    @pl.when(j == 0)
    def _():
        v_s[0:POOL_HALO, :] = jnp.zeros((POOL_HALO, v_s.shape[1]), v_s.dtype)

    x = x_ref[0]
    h = _modulated_norm(x, mod_ref, g_ref)
    proj = jnp.dot(h, w_in_ref[...], preferred_element_type=jnp.float32)
    v_s[POOL_HALO:POOL_HALO + tm, :] = proj[:, :D_MODEL]
    gate = _silu(proj[:, D_MODEL:])

    t = lax.broadcasted_iota(jnp.int32, (tm, POOL_GROUP_DIM), 0) + j * tm
    mixed = []
    for gi, w in enumerate(POOL_WINDOWS):
        cols = slice(gi * POOL_GROUP_DIM, (gi + 1) * POOL_GROUP_DIM)
        cur = v_s[POOL_HALO:POOL_HALO + tm, cols]
        win = v_s[:, cols]
        span = 1
        while span < w:
            win = win + pltpu.roll(win, span, 0)
            span *= 2
        win = win[POOL_HALO:]
        count = jnp.minimum(t + 1, w).astype(jnp.float32)
        pooled = win / count - cur
        mixed.append(jnp.dot(pooled.astype(jnp.bfloat16), w_grp_ref[gi],
                             preferred_element_type=jnp.float32))
    mixed = jnp.concatenate(mixed, axis=1) * scale_ref[...]
    out = jnp.dot((mixed * gate).astype(jnp.bfloat16), w_out_ref[...],
                  preferred_element_type=jnp.float32)
    o_ref[0] = x + mod_ref[0, 2:3, :] * out

    v_s[0:POOL_HALO, :] = v_s[tm:tm + POOL_HALO, :]


def _pool_layer(x, mod, norm_g, w_in, w_group, scale, w_out):
    b, s, d = x.shape
    tm = POOL_TOKEN_TILE
    const = lambda i, j: (0, 0)
    return pl.pallas_call(
        _pool_kernel,
        grid=(b, s // tm),
        in_specs=[
            pl.BlockSpec((1, tm, d), lambda i, j: (i, j, 0)),
            pl.BlockSpec((1, 3, d), lambda i, j: (i, 0, 0)),
            pl.BlockSpec((1, d), const),
            pl.BlockSpec((d, 2 * D_MODEL), const),
            pl.BlockSpec(w_group.shape, lambda i, j: (0, 0, 0)),
            pl.BlockSpec((1, d), const),
            pl.BlockSpec((D_MODEL, d), const),
        ],
        out_specs=pl.BlockSpec((1, tm, d), lambda i, j: (i, j, 0)),
        out_shape=jax.ShapeDtypeStruct(x.shape, x.dtype),
        scratch_shapes=[pltpu.VMEM((POOL_HALO + tm, D_MODEL), jnp.float32)],
        compiler_params=pltpu.CompilerParams(
            dimension_semantics=("arbitrary", "arbitrary"), vmem_limit_bytes=VMEM_LIMIT),
        name="pool_layer",
    )(x, mod, norm_g.reshape(1, d), w_in.astype(jnp.bfloat16), w_group.astype(jnp.bfloat16),
      scale.reshape(1, d), w_out.astype(jnp.bfloat16))


def kernel(x, c, positions, ada_w, ada_b, norm_g, attn_w_in, attn_q_norm, attn_k_norm, attn_sinks,
           attn_w_out, pool_w_in, pool_w_group, pool_scale, pool_w_out):
    depth = ada_w.shape[0]
    b = x.shape[0]
    mod = _adaln(c, ada_w, ada_b).reshape(depth, b, 3, D_MODEL)
    for i in range(depth):
        jl = i // 2
        if i % 2 == 0:
            x = _attn_layer(x, mod[i], norm_g[i], positions, attn_w_in[jl], attn_q_norm[jl],
                            attn_k_norm[jl], attn_sinks[jl], attn_w_out[jl])
        else:
            x = _pool_layer(x, mod[i], norm_g[i], pool_w_in[jl], pool_w_group[jl],
                            pool_scale[jl], pool_w_out[jl])
    return x
```

```python
import math

import jax
import jax.numpy as jnp
from jax import lax
from jax.experimental import pallas as pl
from jax.experimental.pallas import tpu as pltpu

D_MODEL = 1024
HEAD_DIM = 64
N_HEADS = 16
N_KV_HEADS = 4
GROUP = N_HEADS // N_KV_HEADS
Q_WIDTH = N_HEADS * HEAD_DIM
KV_WIDTH = N_KV_HEADS * HEAD_DIM
ATTN_IN_WIDTH = 2 * Q_WIDTH + 2 * KV_WIDTH
BLOCK = 128
ROT_DIM = HEAD_DIM // 4
ROT_HALF = ROT_DIM // 2
ROPE_THETA = 500000.0
POOL_WINDOWS = (2, 4, 8, 16)
POOL_GROUP_DIM = D_MODEL // len(POOL_WINDOWS)
POOL_HALO = 16
NORM_EPS = 1e-6
LANES = 128
BF16_ROWS = 16
V_PAD = BF16_ROWS
V_ROWS = HEAD_DIM + V_PAD
LOG2E = math.log2(math.e)

ATTN_HALF_TILE = 512
PROJ_CHUNK = 256
POOL_TOKEN_TILE = 512
VMEM_LIMIT = 56 * 1024 * 1024

_NT = (((1,), (1,)), ((), ()))
_TN = (((0,), (0,)), ((), ()))


def _silu(x):
    return x / (1.0 + jnp.exp(-x))


def _adaln_kernel(c_ref, w_ref, b_ref, o_ref):
    a = _silu(c_ref[...]).astype(jnp.bfloat16)
    w = w_ref[0].astype(jnp.bfloat16)
    o_ref[0] = jnp.dot(a, w, preferred_element_type=jnp.float32) + b_ref[0]


def _adaln(c, ada_w, ada_b):
    depth, d, n = ada_w.shape
    b = c.shape[0]
    tn = 1024
    return pl.pallas_call(
        _adaln_kernel,
        grid=(depth, n // tn),
        in_specs=[
            pl.BlockSpec((b, d), lambda l, j: (0, 0)),
            pl.BlockSpec((1, d, tn), lambda l, j: (l, 0, j)),
            pl.BlockSpec((1, 1, tn), lambda l, j: (l, 0, j)),
        ],
        out_specs=pl.BlockSpec((1, b, tn), lambda l, j: (l, 0, j)),
        out_shape=jax.ShapeDtypeStruct((depth, b, n), jnp.float32),
        compiler_params=pltpu.CompilerParams(
            dimension_semantics=("arbitrary", "arbitrary"), vmem_limit_bytes=VMEM_LIMIT),
        name="adaln_mod",
    )(c, ada_w, ada_b.reshape(depth, 1, n))


def _modulated_norm(x, mod_ref, g_ref):
    ms = jnp.mean(x * x, axis=-1, keepdims=True)
    gain = g_ref[...] * (1.0 + mod_ref[0, 1:2, :])
    h = x * lax.rsqrt(ms + NORM_EPS) * gain + mod_ref[0, 0:1, :]
    return h.astype(jnp.bfloat16)


def _attn_kernel(x_ref, mod_ref, g_ref, pos_ref, invf_ref, w_in_ref, w_out_ref,
                 qg_ref, kg_ref, sink_ref, o_ref,
                 qt_s, k_s, vt_s, gt_s, ogt_s, w_in_t_s, w_out_s):
    tm = ATTN_HALF_TILE
    j = pl.program_id(1)

    @pl.when((pl.program_id(0) == 0) & (j == 0))
    def _():
        for c in range(ATTN_IN_WIDTH // LANES):
            cols = slice(c * LANES, (c + 1) * LANES)
            w_in_t_s[cols, :] = w_in_ref[0, :, cols].T.astype(jnp.bfloat16)
        for c in range(Q_WIDTH // LANES):
            rows = slice(c * LANES, (c + 1) * LANES)
            w_out_s[rows, :] = w_out_ref[0, rows, :].astype(jnp.bfloat16)

    @pl.when(j == 0)
    def _():
        k_s[0:BLOCK, :] = jnp.zeros((BLOCK, k_s.shape[1]), k_s.dtype)
        vt_s[:, 0:BLOCK] = jnp.zeros((vt_s.shape[0], BLOCK), vt_s.dtype)

    kk = lax.broadcasted_iota(jnp.int32, (BLOCK, 2 * BLOCK), 0)
    qq = lax.broadcasted_iota(jnp.int32, (BLOCK, 2 * BLOCK), 1) & (BLOCK - 1)
    use_cur = kk <= qq
    prev_bias = jnp.where(j > 0, 0.0, -jnp.inf)
    first_head = lax.broadcasted_iota(jnp.int32, (1, 2 * BLOCK), 1) < BLOCK
    zero = jnp.zeros((HEAD_DIM, BLOCK), jnp.bfloat16)
    zero_p = jnp.zeros((BLOCK, 2 * BLOCK), jnp.bfloat16)
    ones = jnp.ones((V_PAD, tm), jnp.bfloat16)
    n_pairs = N_HEADS // 2

    def projection(hf):
        toks = slice(hf * tm, (hf + 1) * tm)
        krows = slice(BLOCK + hf * tm, BLOCK + (hf + 1) * tm)
        st = {}

        def norm():
            st["h"] = _modulated_norm(x_ref[0, toks, :], mod_ref, g_ref)
            ang = pos_ref[0, :, toks].astype(jnp.float32) * invf_ref[...]
            st["cos"], st["sin"] = jnp.cos(ang), jnp.sin(ang)

        def norm_rope(t, gain):
            ssq = jnp.sum(t * t, axis=0, keepdims=True)
            tn = t * lax.rsqrt(ssq * (1.0 / HEAD_DIM) + NORM_EPS) * gain
            x1, x2 = tn[0:ROT_HALF], tn[ROT_HALF:ROT_DIM]
            cos, sin = st["cos"], st["sin"]
            return jnp.concatenate([x1 * cos - x2 * sin, x2 * cos + x1 * sin, tn[ROT_DIM:]], axis=0)

        def chunk(c):
            rows = slice(c * PROJ_CHUNK, (c + 1) * PROJ_CHUNK)
            pt = lax.dot_general(w_in_t_s[rows, :], st["h"], _NT,
                                 preferred_element_type=jnp.float32)
            heads = [pt[i * HEAD_DIM:(i + 1) * HEAD_DIM] for i in range(PROJ_CHUNK // HEAD_DIM)]
            if c < Q_WIDTH // PROJ_CHUNK:
                for i, t in enumerate(heads):
                    hd = c * (PROJ_CHUNK // HEAD_DIM) + i
                    qt_s[hd * HEAD_DIM:(hd + 1) * HEAD_DIM, toks] = (
                        norm_rope(t, qg_ref[...]).astype(jnp.bfloat16))
            elif c == Q_WIDTH // PROJ_CHUNK:
                for kh, t in enumerate(heads):
                    kt = norm_rope(t, kg_ref[...])
                    k_s[krows, kh * LANES:(kh + 1) * LANES] = (
                        jnp.concatenate([kt, kt], axis=0).T.astype(jnp.bfloat16))
            elif c == Q_WIDTH // PROJ_CHUNK + 1:
                for kh, t in enumerate(heads):
                    vt_s[kh * V_ROWS:kh * V_ROWS + HEAD_DIM, krows] = t.astype(jnp.bfloat16)
                    vt_s[kh * V_ROWS + HEAD_DIM:(kh + 1) * V_ROWS, krows] = ones
            else:
                g0 = (c - Q_WIDTH // PROJ_CHUNK - 2) * PROJ_CHUNK
                gt_s[g0:g0 + PROJ_CHUNK, toks] = _silu(pt)

        return [norm] + [(lambda c=c: chunk(c)) for c in range(ATTN_IN_WIDTH // PROJ_CHUNK)]

    def scores(qb):
        qcols = slice(qb * BLOCK, (qb + 1) * BLOCK)
        keys = slice(qb * BLOCK, (qb + 2) * BLOCK)
        out = []
        for pr in range(n_pairs):
            kh = pr // (GROUP // 2)
            kd = k_s[keys, kh * LANES:(kh + 1) * LANES]
            r0 = slice(2 * pr * HEAD_DIM, (2 * pr + 1) * HEAD_DIM)
            r1 = slice((2 * pr + 1) * HEAD_DIM, (2 * pr + 2) * HEAD_DIM)
            qz = jnp.concatenate([jnp.concatenate([qt_s[r0, qcols], zero], axis=1),
                                  jnp.concatenate([zero, qt_s[r1, qcols]], axis=1)], axis=0)
            s = jnp.dot(kd, qz, preferred_element_type=jnp.float32)
            s_prev = s[:BLOCK] + prev_bias if qb == 0 else s[:BLOCK]
            out.append(jnp.where(use_cur, s[BLOCK:], s_prev))
        return out

    def finish(qb, s_list):
        qcols = slice(qb * BLOCK, (qb + 1) * BLOCK)
        keys = slice(qb * BLOCK, (qb + 2) * BLOCK)
        for pr, s in enumerate(s_list):
            kh = pr // (GROUP // 2)
            vt = vt_s[kh * V_ROWS:(kh + 1) * V_ROWS, keys]
            r0 = slice(2 * pr * HEAD_DIM, (2 * pr + 1) * HEAD_DIM)
            r1 = slice((2 * pr + 1) * HEAD_DIM, (2 * pr + 2) * HEAD_DIM)
            sink = jnp.where(first_head, sink_ref[2 * pr], sink_ref[2 * pr + 1])
            m = jnp.maximum(jnp.max(s, axis=0, keepdims=True), sink)
            p = jnp.exp2(s - m).astype(jnp.bfloat16)
            p2 = jnp.concatenate([jnp.where(use_cur, zero_p, p), jnp.where(use_cur, p, zero_p)], axis=0)
            o = jnp.dot(vt, p2, preferred_element_type=jnp.float32)
            denom = o[HEAD_DIM:HEAD_DIM + 1] + jnp.exp2(sink - m)
            o = o[:HEAD_DIM] * (1.0 / denom)
            ogt_s[r0, qcols] = (o[:, :BLOCK] * gt_s[r0, qcols]).astype(jnp.bfloat16)
            ogt_s[r1, qcols] = (o[:, BLOCK:] * gt_s[r1, qcols]).astype(jnp.bfloat16)

    def attention(hf):
        blocks = range(hf * (tm // BLOCK), (hf + 1) * (tm // BLOCK))
        st = {}

        def first():
            st["s"] = scores(blocks[0])

        def step(qb):
            s_cur = st["s"]
            if qb + 1 in blocks:
                st["s"] = scores(qb + 1)
            finish(qb, s_cur)

        return [first] + [(lambda qb=qb: step(qb)) for qb in blocks]

    def out_projection(hf):
        toks = slice(hf * tm, (hf + 1) * tm)

        def run():
            out = lax.dot_general(ogt_s[:, toks], w_out_s[...], _TN,
                                  preferred_element_type=jnp.float32)
            o_ref[0, toks, :] = x_ref[0, toks, :] + mod_ref[0, 2:3, :] * out

        return [run]

    def interleave(a, b):
        done_a = done_b = 0
        while done_a < len(a) or done_b < len(b):
            if done_b >= len(b) or (done_a < len(a) and done_a * len(b) <= done_b * len(a)):
                a[done_a]()
                done_a += 1
            else:
                b[done_b]()
                done_b += 1

    proj0, proj1 = projection(0), projection(1)
    for f in proj0:
        f()
    proj1[0]()
    interleave(attention(0), proj1[1:])
    interleave(attention(1), out_projection(0))
    out_projection(1)[0]()

    step_tokens = 2 * tm
    k_s[0:BLOCK, :] = k_s[step_tokens:step_tokens + BLOCK, :]
    vt_s[:, 0:BLOCK] = vt_s[:, step_tokens:step_tokens + BLOCK]


def _attn_layer(x, mod, norm_g, positions, w_in_all, q_gain, k_gain, sinks, w_out_all, layer):
    b, s, d = x.shape
    tm = ATTN_HALF_TILE
    step = 2 * tm
    inv_freq = ROPE_THETA ** (-jnp.arange(ROT_HALF, dtype=jnp.float32) * 2.0 / ROT_DIM)
    invf = jnp.broadcast_to(inv_freq[:, None], (ROT_HALF, tm))
    qg = jnp.broadcast_to((q_gain * (HEAD_DIM ** -0.5 * LOG2E))[:, None], (HEAD_DIM, tm))
    kg = jnp.broadcast_to(k_gain[:, None], (HEAD_DIM, tm))
    const = lambda i, j: (0, 0)
    return pl.pallas_call(
        _attn_kernel,
        grid=(b, s // step),
        in_specs=[
            pl.BlockSpec((1, step, d), lambda i, j: (i, j, 0)),
            pl.BlockSpec((1, 3, d), lambda i, j: (i, 0, 0)),
            pl.BlockSpec((1, d), const),
            pl.BlockSpec((1, 1, step), lambda i, j: (i, 0, j)),
            pl.BlockSpec((ROT_HALF, tm), const),
            pl.BlockSpec((1, d, ATTN_IN_WIDTH), lambda i, j: (layer, 0, 0)),
            pl.BlockSpec((1, Q_WIDTH, d), lambda i, j: (layer, 0, 0)),
            pl.BlockSpec((HEAD_DIM, tm), const),
            pl.BlockSpec((HEAD_DIM, tm), const),
            pl.BlockSpec(memory_space=pltpu.SMEM),
        ],
        out_specs=pl.BlockSpec((1, step, d), lambda i, j: (i, j, 0)),
        out_shape=jax.ShapeDtypeStruct(x.shape, x.dtype),
        scratch_shapes=[
            pltpu.VMEM((Q_WIDTH, step), jnp.bfloat16),
            pltpu.VMEM((BLOCK + step, N_KV_HEADS * LANES), jnp.bfloat16),
            pltpu.VMEM((N_KV_HEADS * V_ROWS, BLOCK + step), jnp.bfloat16),
            pltpu.VMEM((Q_WIDTH, step), jnp.float32),
            pltpu.VMEM((Q_WIDTH, step), jnp.bfloat16),
            pltpu.VMEM((ATTN_IN_WIDTH, d), jnp.bfloat16),
            pltpu.VMEM((Q_WIDTH, d), jnp.bfloat16),
        ],
        compiler_params=pltpu.CompilerParams(
            dimension_semantics=("arbitrary", "arbitrary"), vmem_limit_bytes=VMEM_LIMIT),
        name="attn_layer",
    )(x, mod, norm_g.reshape(1, d), positions.reshape(b, 1, s), invf,
      w_in_all, w_out_all, qg, kg, sinks * LOG2E)


def _pool_kernel(x_ref, mod_ref, g_ref, w_in_ref, w_grp_ref, scale_ref, w_out_ref, o_ref,
                 v_s, w_in_s, w_grp_s, w_out_s):
    tm = x_ref.shape[1]
    j = pl.program_id(1)

    @pl.when((pl.program_id(0) == 0) & (j == 0))
    def _():
        for c in range(D_MODEL // LANES):
            rows = slice(c * LANES, (c + 1) * LANES)
            w_in_s[rows, :] = w_in_ref[0, rows, :].astype(jnp.bfloat16)
            w_out_s[rows, :] = w_out_ref[0, rows, :].astype(jnp.bfloat16)
        for gi in range(len(POOL_WINDOWS)):
            w_grp_s[gi] = w_grp_ref[0, gi].astype(jnp.bfloat16)

    @pl.when(j == 0)
    def _():
        v_s[0:POOL_HALO, :] = jnp.zeros((POOL_HALO, v_s.shape[1]), v_s.dtype)

    x = x_ref[0]
    h = _modulated_norm(x, mod_ref, g_ref)
    proj = jnp.dot(h, w_in_s[...], preferred_element_type=jnp.float32)
    v_s[POOL_HALO:POOL_HALO + tm, :] = proj[:, :D_MODEL]
    gate = _silu(proj[:, D_MODEL:])

    t = lax.broadcasted_iota(jnp.int32, (tm, POOL_GROUP_DIM), 0) + j * tm
    mixed = []
    for gi, w in enumerate(POOL_WINDOWS):
        cols = slice(gi * POOL_GROUP_DIM, (gi + 1) * POOL_GROUP_DIM)
        cur = v_s[POOL_HALO:POOL_HALO + tm, cols]
        win = v_s[:, cols]
        span = 1
        while span < w:
            win = win + pltpu.roll(win, span, 0)
            span *= 2
        win = win[POOL_HALO:]
        count = jnp.minimum(t + 1, w).astype(jnp.float32)
        pooled = win / count - cur
        mixed.append(jnp.dot(pooled.astype(jnp.bfloat16), w_grp_s[gi],
                             preferred_element_type=jnp.float32))
    mixed = jnp.concatenate(mixed, axis=1) * scale_ref[...]
    out = jnp.dot((mixed * gate).astype(jnp.bfloat16), w_out_s[...],
                  preferred_element_type=jnp.float32)
    o_ref[0] = x + mod_ref[0, 2:3, :] * out

    v_s[0:POOL_HALO, :] = v_s[tm:tm + POOL_HALO, :]


def _pool_layer(x, mod, norm_g, w_in_all, w_group_all, scale, w_out_all, layer):
    b, s, d = x.shape
    tm = POOL_TOKEN_TILE
    n_grp = len(POOL_WINDOWS)
    const = lambda i, j: (0, 0)
    return pl.pallas_call(
        _pool_kernel,
        grid=(b, s // tm),
        in_specs=[
            pl.BlockSpec((1, tm, d), lambda i, j: (i, j, 0)),
            pl.BlockSpec((1, 3, d), lambda i, j: (i, 0, 0)),
            pl.BlockSpec((1, d), const),
            pl.BlockSpec((1, d, 2 * D_MODEL), lambda i, j: (layer, 0, 0)),
            pl.BlockSpec((1, n_grp, POOL_GROUP_DIM, POOL_GROUP_DIM), lambda i, j: (layer, 0, 0, 0)),
            pl.BlockSpec((1, d), const),
            pl.BlockSpec((1, D_MODEL, d), lambda i, j: (layer, 0, 0)),
        ],
        out_specs=pl.BlockSpec((1, tm, d), lambda i, j: (i, j, 0)),
        out_shape=jax.ShapeDtypeStruct(x.shape, x.dtype),
        scratch_shapes=[
            pltpu.VMEM((POOL_HALO + tm, D_MODEL), jnp.float32),
            pltpu.VMEM((d, 2 * D_MODEL), jnp.bfloat16),
            pltpu.VMEM((n_grp, POOL_GROUP_DIM, POOL_GROUP_DIM), jnp.bfloat16),
            pltpu.VMEM((D_MODEL, d), jnp.bfloat16),
        ],
        compiler_params=pltpu.CompilerParams(
            dimension_semantics=("arbitrary", "arbitrary"), vmem_limit_bytes=VMEM_LIMIT),
        name="pool_layer",
    )(x, mod, norm_g.reshape(1, d), w_in_all, w_group_all, scale.reshape(1, d), w_out_all)


def kernel(x, c, positions, ada_w, ada_b, norm_g, attn_w_in, attn_q_norm, attn_k_norm, attn_sinks,
           attn_w_out, pool_w_in, pool_w_group, pool_scale, pool_w_out):
    depth = ada_w.shape[0]
    b = x.shape[0]
    mod = _adaln(c, ada_w, ada_b).reshape(depth, b, 3, D_MODEL)
    for i in range(depth):
        jl = i // 2
        if i % 2 == 0:
            x = _attn_layer(x, mod[i], norm_g[i], positions, attn_w_in, attn_q_norm[jl],
                            attn_k_norm[jl], attn_sinks[jl], attn_w_out, jl)
        else:
            x = _pool_layer(x, mod[i], norm_g[i], pool_w_in, pool_w_group,
                            pool_scale[jl], pool_w_out, jl)
    return x
```

```python
import math

import jax
import jax.numpy as jnp
from jax import lax
from jax.experimental import pallas as pl
from jax.experimental.pallas import tpu as pltpu

D_MODEL = 1024
HEAD_DIM = 64
N_HEADS = 16
N_KV_HEADS = 4
GROUP = N_HEADS // N_KV_HEADS
Q_WIDTH = N_HEADS * HEAD_DIM
KV_WIDTH = N_KV_HEADS * HEAD_DIM
ATTN_IN_WIDTH = 2 * Q_WIDTH + 2 * KV_WIDTH
BLOCK = 128
ROT_DIM = HEAD_DIM // 4
ROT_HALF = ROT_DIM // 2
ROPE_THETA = 500000.0
POOL_WINDOWS = (2, 4, 8, 16)
POOL_GROUP_DIM = D_MODEL // len(POOL_WINDOWS)
POOL_HALO = 16
NORM_EPS = 1e-6
LANES = 128
BF16_ROWS = 16
V_PAD = BF16_ROWS
V_ROWS = HEAD_DIM + V_PAD
LOG2E = math.log2(math.e)

ATTN_HALF_TILE = 512
PROJ_CHUNK = 256
POOL_HALF_TILE = 512
VMEM_LIMIT = 56 * 1024 * 1024

_NT = (((1,), (1,)), ((), ()))
_TN = (((0,), (0,)), ((), ()))


def _silu(x):
    return x / (1.0 + jnp.exp(-x))


def _adaln_kernel(c_ref, w_ref, b_ref, o_ref):
    a = _silu(c_ref[...]).astype(jnp.bfloat16)
    w = w_ref[0].astype(jnp.bfloat16)
    o_ref[0] = jnp.dot(a, w, preferred_element_type=jnp.float32) + b_ref[0]


def _adaln(c, ada_w, ada_b):
    depth, d, n = ada_w.shape
    b = c.shape[0]
    tn = 1024
    return pl.pallas_call(
        _adaln_kernel,
        grid=(depth, n // tn),
        in_specs=[
            pl.BlockSpec((b, d), lambda l, j: (0, 0)),
            pl.BlockSpec((1, d, tn), lambda l, j: (l, 0, j)),
            pl.BlockSpec((1, 1, tn), lambda l, j: (l, 0, j)),
        ],
        out_specs=pl.BlockSpec((1, b, tn), lambda l, j: (l, 0, j)),
        out_shape=jax.ShapeDtypeStruct((depth, b, n), jnp.float32),
        compiler_params=pltpu.CompilerParams(
            dimension_semantics=("arbitrary", "arbitrary"), vmem_limit_bytes=VMEM_LIMIT),
        name="adaln_mod",
    )(c, ada_w, ada_b.reshape(depth, 1, n))


def _modulated_norm(x, mod_ref, g_ref):
    ms = jnp.mean(x * x, axis=-1, keepdims=True)
    gain = g_ref[...] * (1.0 + mod_ref[0, 1:2, :])
    h = x * lax.rsqrt(ms + NORM_EPS) * gain + mod_ref[0, 0:1, :]
    return h.astype(jnp.bfloat16)


def _attn_kernel(x_ref, mod_ref, g_ref, pos_ref, invf_ref, w_in_ref, w_out_ref,
                 qg_ref, kg_ref, sink_ref, o_ref,
                 qt_s, k_s, vt_s, gt_s, ogt_s, w_in_t_s, w_out_s):
    tm = ATTN_HALF_TILE
    j = pl.program_id(1)

    @pl.when((pl.program_id(0) == 0) & (j == 0))
    def _():
        for c in range(ATTN_IN_WIDTH // LANES):
            cols = slice(c * LANES, (c + 1) * LANES)
            w_in_t_s[cols, :] = w_in_ref[0, :, cols].T.astype(jnp.bfloat16)
        for c in range(Q_WIDTH // LANES):
            rows = slice(c * LANES, (c + 1) * LANES)
            w_out_s[rows, :] = w_out_ref[0, rows, :].astype(jnp.bfloat16)

    @pl.when(j == 0)
    def _():
        k_s[0:BLOCK, :] = jnp.zeros((BLOCK, k_s.shape[1]), k_s.dtype)
        vt_s[:, 0:BLOCK] = jnp.zeros((vt_s.shape[0], BLOCK), vt_s.dtype)

    kk = lax.broadcasted_iota(jnp.int32, (BLOCK, 2 * BLOCK), 0)
    qq = lax.broadcasted_iota(jnp.int32, (BLOCK, 2 * BLOCK), 1) & (BLOCK - 1)
    use_cur = kk <= qq
    prev_bias = jnp.where(j > 0, 0.0, -jnp.inf)
    first_head = lax.broadcasted_iota(jnp.int32, (1, 2 * BLOCK), 1) < BLOCK
    zero = jnp.zeros((HEAD_DIM, BLOCK), jnp.bfloat16)
    zero_p = jnp.zeros((BLOCK, 2 * BLOCK), jnp.bfloat16)
    ones = jnp.ones((V_PAD, tm), jnp.bfloat16)
    n_pairs = N_HEADS // 2

    def projection(hf):
        toks = slice(hf * tm, (hf + 1) * tm)
        krows = slice(BLOCK + hf * tm, BLOCK + (hf + 1) * tm)
        st = {}

        def norm():
            st["h"] = _modulated_norm(x_ref[0, toks, :], mod_ref, g_ref)
            ang = pos_ref[0, :, toks].astype(jnp.float32) * invf_ref[...]
            st["cos"], st["sin"] = jnp.cos(ang), jnp.sin(ang)

        def norm_rope(t, gain):
            ssq = jnp.sum(t * t, axis=0, keepdims=True)
            tn = t * lax.rsqrt(ssq * (1.0 / HEAD_DIM) + NORM_EPS) * gain
            x1, x2 = tn[0:ROT_HALF], tn[ROT_HALF:ROT_DIM]
            cos, sin = st["cos"], st["sin"]
            return jnp.concatenate([x1 * cos - x2 * sin, x2 * cos + x1 * sin, tn[ROT_DIM:]], axis=0)

        def chunk(c):
            row0 = c * PROJ_CHUNK
            pt = lax.dot_general(w_in_t_s[row0:row0 + PROJ_CHUNK, :], st["h"], _NT,
                                 preferred_element_type=jnp.float32)
            if row0 >= Q_WIDTH + 2 * KV_WIDTH:
                g0 = row0 - (Q_WIDTH + 2 * KV_WIDTH)
                gt_s[g0:g0 + PROJ_CHUNK, toks] = _silu(pt)
                return
            for i in range(PROJ_CHUNK // HEAD_DIM):
                t = pt[i * HEAD_DIM:(i + 1) * HEAD_DIM]
                hd = row0 // HEAD_DIM + i
                if hd < N_HEADS:
                    qt_s[hd * HEAD_DIM:(hd + 1) * HEAD_DIM, toks] = (
                        norm_rope(t, qg_ref[...]).astype(jnp.bfloat16))
                elif hd < N_HEADS + N_KV_HEADS:
                    kh = hd - N_HEADS
                    kt = norm_rope(t, kg_ref[...])
                    k_s[krows, kh * LANES:(kh + 1) * LANES] = (
                        jnp.concatenate([kt, kt], axis=0).T.astype(jnp.bfloat16))
                else:
                    kh = hd - N_HEADS - N_KV_HEADS
                    vt_s[kh * V_ROWS:kh * V_ROWS + HEAD_DIM, krows] = t.astype(jnp.bfloat16)
                    vt_s[kh * V_ROWS + HEAD_DIM:(kh + 1) * V_ROWS, krows] = ones

        return [norm] + [(lambda c=c: chunk(c)) for c in range(ATTN_IN_WIDTH // PROJ_CHUNK)]

    def scores(qb):
        qcols = slice(qb * BLOCK, (qb + 1) * BLOCK)
        keys = slice(qb * BLOCK, (qb + 2) * BLOCK)
        out = []
        for pr in range(n_pairs):
            kh = pr // (GROUP // 2)
            kd = k_s[keys, kh * LANES:(kh + 1) * LANES]
            r0 = slice(2 * pr * HEAD_DIM, (2 * pr + 1) * HEAD_DIM)
            r1 = slice((2 * pr + 1) * HEAD_DIM, (2 * pr + 2) * HEAD_DIM)
            qz = jnp.concatenate([jnp.concatenate([qt_s[r0, qcols], zero], axis=1),
                                  jnp.concatenate([zero, qt_s[r1, qcols]], axis=1)], axis=0)
            s = jnp.dot(kd, qz, preferred_element_type=jnp.float32)
            s_prev = s[:BLOCK] + prev_bias if qb == 0 else s[:BLOCK]
            out.append(jnp.where(use_cur, s[BLOCK:], s_prev))
        return out

    def finish(qb, s_list):
        qcols = slice(qb * BLOCK, (qb + 1) * BLOCK)
        keys = slice(qb * BLOCK, (qb + 2) * BLOCK)
        for pr, s in enumerate(s_list):
            kh = pr // (GROUP // 2)
            vt = vt_s[kh * V_ROWS:(kh + 1) * V_ROWS, keys]
            r0 = slice(2 * pr * HEAD_DIM, (2 * pr + 1) * HEAD_DIM)
            r1 = slice((2 * pr + 1) * HEAD_DIM, (2 * pr + 2) * HEAD_DIM)
            sink = jnp.where(first_head, sink_ref[2 * pr], sink_ref[2 * pr + 1])
            m = jnp.maximum(jnp.max(s, axis=0, keepdims=True), sink)
            p = jnp.exp2(s - m).astype(jnp.bfloat16)
            p2 = jnp.concatenate([jnp.where(use_cur, zero_p, p), jnp.where(use_cur, p, zero_p)], axis=0)
            o = jnp.dot(vt, p2, preferred_element_type=jnp.float32)
            denom = o[HEAD_DIM:HEAD_DIM + 1] + jnp.exp2(sink - m)
            o = o[:HEAD_DIM] * (1.0 / denom)
            ogt_s[r0, qcols] = (o[:, :BLOCK] * gt_s[r0, qcols]).astype(jnp.bfloat16)
            ogt_s[r1, qcols] = (o[:, BLOCK:] * gt_s[r1, qcols]).astype(jnp.bfloat16)

    def attention(hf):
        blocks = range(hf * (tm // BLOCK), (hf + 1) * (tm // BLOCK))
        st = {}

        def first():
            st["s"] = scores(blocks[0])

        def step(qb):
            s_cur = st["s"]
            if qb + 1 in blocks:
                st["s"] = scores(qb + 1)
            finish(qb, s_cur)

        return [first] + [(lambda qb=qb: step(qb)) for qb in blocks]

    def out_projection(hf):
        toks = slice(hf * tm, (hf + 1) * tm)

        def run():
            out = lax.dot_general(ogt_s[:, toks], w_out_s[...], _TN,
                                  preferred_element_type=jnp.float32)
            o_ref[0, toks, :] = x_ref[0, toks, :] + mod_ref[0, 2:3, :] * out

        return [run]

    def interleave(a, b):
        done_a = done_b = 0
        while done_a < len(a) or done_b < len(b):
            if done_b >= len(b) or (done_a < len(a) and done_a * len(b) <= done_b * len(a)):
                a[done_a]()
                done_a += 1
            else:
                b[done_b]()
                done_b += 1

    proj0, proj1 = projection(0), projection(1)
    for f in proj0:
        f()
    proj1[0]()
    interleave(attention(0), proj1[1:])
    interleave(attention(1), out_projection(0))
    out_projection(1)[0]()

    step_tokens = 2 * tm
    k_s[0:BLOCK, :] = k_s[step_tokens:step_tokens + BLOCK, :]
    vt_s[:, 0:BLOCK] = vt_s[:, step_tokens:step_tokens + BLOCK]


def _attn_layer(x, mod, norm_g, positions, w_in_all, q_gain, k_gain, sinks, w_out_all, layer):
    b, s, d = x.shape
    tm = ATTN_HALF_TILE
    step = 2 * tm
    inv_freq = ROPE_THETA ** (-jnp.arange(ROT_HALF, dtype=jnp.float32) * 2.0 / ROT_DIM)
    invf = jnp.broadcast_to(inv_freq[:, None], (ROT_HALF, tm))
    qg = jnp.broadcast_to((q_gain * (HEAD_DIM ** -0.5 * LOG2E))[:, None], (HEAD_DIM, tm))
    kg = jnp.broadcast_to(k_gain[:, None], (HEAD_DIM, tm))
    const = lambda i, j: (0, 0)
    return pl.pallas_call(
        _attn_kernel,
        grid=(b, s // step),
        in_specs=[
            pl.BlockSpec((1, step, d), lambda i, j: (i, j, 0)),
            pl.BlockSpec((1, 3, d), lambda i, j: (i, 0, 0)),
            pl.BlockSpec((1, d), const),
            pl.BlockSpec((1, 1, step), lambda i, j: (i, 0, j)),
            pl.BlockSpec((ROT_HALF, tm), const),
            pl.BlockSpec((1, d, ATTN_IN_WIDTH), lambda i, j: (layer, 0, 0)),
            pl.BlockSpec((1, Q_WIDTH, d), lambda i, j: (layer, 0, 0)),
            pl.BlockSpec((HEAD_DIM, tm), const),
            pl.BlockSpec((HEAD_DIM, tm), const),
            pl.BlockSpec(memory_space=pltpu.SMEM),
        ],
        out_specs=pl.BlockSpec((1, step, d), lambda i, j: (i, j, 0)),
        out_shape=jax.ShapeDtypeStruct(x.shape, x.dtype),
        scratch_shapes=[
            pltpu.VMEM((Q_WIDTH, step), jnp.bfloat16),
            pltpu.VMEM((BLOCK + step, N_KV_HEADS * LANES), jnp.bfloat16),
            pltpu.VMEM((N_KV_HEADS * V_ROWS, BLOCK + step), jnp.bfloat16),
            pltpu.VMEM((Q_WIDTH, step), jnp.float32),
            pltpu.VMEM((Q_WIDTH, step), jnp.bfloat16),
            pltpu.VMEM((ATTN_IN_WIDTH, d), jnp.bfloat16),
            pltpu.VMEM((Q_WIDTH, d), jnp.bfloat16),
        ],
        compiler_params=pltpu.CompilerParams(
            dimension_semantics=("arbitrary", "arbitrary"), vmem_limit_bytes=VMEM_LIMIT),
        name="attn_layer",
    )(x, mod, norm_g.reshape(1, d), positions.reshape(b, 1, s), invf,
      w_in_all, w_out_all, qg, kg, sinks * LOG2E)


def _pool_kernel(x_ref, mod_ref, g_ref, w_in_ref, w_grp_ref, scale_ref, w_out_ref, o_ref,
                 v_s, w_in_s, w_grp_s, w_out_s):
    tm = POOL_HALF_TILE
    j = pl.program_id(1)

    @pl.when((pl.program_id(0) == 0) & (j == 0))
    def _():
        for c in range(D_MODEL // LANES):
            rows = slice(c * LANES, (c + 1) * LANES)
            w_in_s[rows, :] = w_in_ref[0, rows, :].astype(jnp.bfloat16)
            w_out_s[rows, :] = w_out_ref[0, rows, :].astype(jnp.bfloat16)
        for gi in range(len(POOL_WINDOWS)):
            w_grp_s[gi] = w_grp_ref[0, gi].astype(jnp.bfloat16)

    @pl.when(j == 0)
    def _():
        v_s[0:POOL_HALO, :] = jnp.zeros((POOL_HALO, v_s.shape[1]), v_s.dtype)

    gates = {}

    def projection(hf):
        toks = slice(hf * tm, (hf + 1) * tm)
        st = {}

        def norm():
            st["h"] = _modulated_norm(x_ref[0, toks, :], mod_ref, g_ref)

        def value_path():
            v_s[POOL_HALO + hf * tm:POOL_HALO + (hf + 1) * tm, :] = jnp.dot(
                st["h"], w_in_s[:, :D_MODEL], preferred_element_type=jnp.float32)

        def gate_path():
            gates[hf] = _silu(jnp.dot(st["h"], w_in_s[:, D_MODEL:],
                                      preferred_element_type=jnp.float32))

        return [norm, value_path, gate_path]

    def mixing(hf):
        toks = slice(hf * tm, (hf + 1) * tm)
        ext = slice(hf * tm, POOL_HALO + (hf + 1) * tm)
        t = lax.broadcasted_iota(jnp.int32, (tm, POOL_GROUP_DIM), 0) + (j * 2 + hf) * tm
        mixed = []

        def group(gi):
            w = POOL_WINDOWS[gi]
            cols = slice(gi * POOL_GROUP_DIM, (gi + 1) * POOL_GROUP_DIM)
            win = v_s[ext, cols]
            cur = win[POOL_HALO:]
            span = 1
            while span < w:
                win = win + pltpu.roll(win, span, 0)
                span *= 2
            count = jnp.minimum(t + 1, w).astype(jnp.float32)
            pooled = win[POOL_HALO:] / count - cur
            mixed.append(jnp.dot(pooled.astype(jnp.bfloat16), w_grp_s[gi],
                                 preferred_element_type=jnp.float32))

        def out():
            m = jnp.concatenate(mixed, axis=1) * scale_ref[...]
            o = jnp.dot((m * gates[hf]).astype(jnp.bfloat16), w_out_s[...],
                        preferred_element_type=jnp.float32)
            o_ref[0, toks, :] = x_ref[0, toks, :] + mod_ref[0, 2:3, :] * o

        return [(lambda gi=gi: group(gi)) for gi in range(len(POOL_WINDOWS))] + [out]

    def interleave(a, b):
        done_a = done_b = 0
        while done_a < len(a) or done_b < len(b):
            if done_b >= len(b) or (done_a < len(a) and done_a * len(b) <= done_b * len(a)):
                a[done_a]()
                done_a += 1
            else:
                b[done_b]()
                done_b += 1

    proj0, proj1 = projection(0), projection(1)
    for f in proj0:
        f()
    proj1[0]()
    interleave(mixing(0), proj1[1:])
    for f in mixing(1):
        f()

    v_s[0:POOL_HALO, :] = v_s[2 * tm:2 * tm + POOL_HALO, :]


def _pool_layer(x, mod, norm_g, w_in_all, w_group_all, scale, w_out_all, layer):
    b, s, d = x.shape
    tm = 2 * POOL_HALF_TILE
    n_grp = len(POOL_WINDOWS)
    const = lambda i, j: (0, 0)
    return pl.pallas_call(
        _pool_kernel,
        grid=(b, s // tm),
        in_specs=[
            pl.BlockSpec((1, tm, d), lambda i, j: (i, j, 0)),
            pl.BlockSpec((1, 3, d), lambda i, j: (i, 0, 0)),
            pl.BlockSpec((1, d), const),
            pl.BlockSpec((1, d, 2 * D_MODEL), lambda i, j: (layer, 0, 0)),
            pl.BlockSpec((1, n_grp, POOL_GROUP_DIM, POOL_GROUP_DIM), lambda i, j: (layer, 0, 0, 0)),
            pl.BlockSpec((1, d), const),
            pl.BlockSpec((1, D_MODEL, d), lambda i, j: (layer, 0, 0)),
        ],
        out_specs=pl.BlockSpec((1, tm, d), lambda i, j: (i, j, 0)),
        out_shape=jax.ShapeDtypeStruct(x.shape, x.dtype),
        scratch_shapes=[
            pltpu.VMEM((POOL_HALO + tm, D_MODEL), jnp.float32),
            pltpu.VMEM((d, 2 * D_MODEL), jnp.bfloat16),
            pltpu.VMEM((n_grp, POOL_GROUP_DIM, POOL_GROUP_DIM), jnp.bfloat16),
            pltpu.VMEM((D_MODEL, d), jnp.bfloat16),
        ],
        compiler_params=pltpu.CompilerParams(
            dimension_semantics=("arbitrary", "arbitrary"), vmem_limit_bytes=VMEM_LIMIT),
        name="pool_layer",
    )(x, mod, norm_g.reshape(1, d), w_in_all, w_group_all, scale.reshape(1, d), w_out_all)


def kernel(x, c, positions, ada_w, ada_b, norm_g, attn_w_in, attn_q_norm, attn_k_norm, attn_sinks,
           attn_w_out, pool_w_in, pool_w_group, pool_scale, pool_w_out):
    depth = ada_w.shape[0]
    b = x.shape[0]
    mod = _adaln(c, ada_w, ada_b).reshape(depth, b, 3, D_MODEL)
    for i in range(depth):
        jl = i // 2
        if i % 2 == 0:
            x = _attn_layer(x, mod[i], norm_g[i], positions, attn_w_in, attn_q_norm[jl],
                            attn_k_norm[jl], attn_sinks[jl], attn_w_out, jl)
        else:
            x = _pool_layer(x, mod[i], norm_g[i], pool_w_in, pool_w_group,
                            pool_scale[jl], pool_w_out, jl)
    return x
```

```python
import functools
import math

import jax
import jax.numpy as jnp
from jax import lax
from jax.experimental import pallas as pl
from jax.experimental.pallas import tpu as pltpu

D_MODEL = 1024
HEAD_DIM = 64
N_HEADS = 16
N_KV_HEADS = 4
GROUP = N_HEADS // N_KV_HEADS
Q_WIDTH = N_HEADS * HEAD_DIM
KV_WIDTH = N_KV_HEADS * HEAD_DIM
ATTN_IN_WIDTH = 2 * Q_WIDTH + 2 * KV_WIDTH
BLOCK = 128
ROT_DIM = HEAD_DIM // 4
ROT_HALF = ROT_DIM // 2
ROPE_THETA = 500000.0
POOL_WINDOWS = (2, 4, 8, 16)
POOL_GROUP_DIM = D_MODEL // len(POOL_WINDOWS)
POOL_HALO = 16
NORM_EPS = 1e-6
LANES = 128
BF16_ROWS = 16
V_PAD = BF16_ROWS
V_ROWS = HEAD_DIM + V_PAD
LOG2E = math.log2(math.e)

ATTN_HALF_TILE = 512
PROJ_CHUNK = 256
POOL_HALF_TILE = 512
VMEM_LIMIT = 56 * 1024 * 1024

_NT = (((1,), (1,)), ((), ()))
_TN = (((0,), (0,)), ((), ()))


def _silu(x):
    hx = 0.5 * x
    return hx + hx * jnp.tanh(hx)


def _adaln_kernel(c_ref, w_ref, b_ref, o_ref):
    a = _silu(c_ref[...]).astype(jnp.bfloat16)
    w = w_ref[0].astype(jnp.bfloat16)
    o_ref[0] = jnp.dot(a, w, preferred_element_type=jnp.float32) + b_ref[0]


def _adaln(c, ada_w, ada_b, depth):
    _, d, n = ada_w.shape
    b = c.shape[0]
    tn = 1024
    return pl.pallas_call(
        _adaln_kernel,
        grid=(depth, n // tn),
        in_specs=[
            pl.BlockSpec((b, d), lambda l, j: (0, 0)),
            pl.BlockSpec((1, d, tn), lambda l, j: (l, 0, j)),
            pl.BlockSpec((1, 1, tn), lambda l, j: (l, 0, j)),
        ],
        out_specs=pl.BlockSpec((1, b, tn), lambda l, j: (l, 0, j)),
        out_shape=jax.ShapeDtypeStruct((depth, b, n), jnp.float32),
        compiler_params=pltpu.CompilerParams(
            dimension_semantics=("arbitrary", "arbitrary"), vmem_limit_bytes=VMEM_LIMIT),
        name="adaln_mod",
    )(c, ada_w, ada_b.reshape(ada_b.shape[0], 1, n))


def _modulated_norm(x, mod_ref, g_ref):
    ms = jnp.mean(x * x, axis=-1, keepdims=True)
    gain = g_ref[...] * (1.0 + mod_ref[0, 1:2, :])
    h = x * lax.rsqrt(ms + NORM_EPS) * gain + mod_ref[0, 0:1, :]
    return h.astype(jnp.bfloat16)


def _attn_kernel(*refs, n_later):
    (x_ref, mod_ref, g_ref, pos_ref, invf_ref, w_in_ref, w_out_ref,
     qg_ref, kg_ref, sink_ref) = refs[:10]
    later = refs[10:10 + (n_later + 2 if n_later else 0)]
    refs = refs[10 + len(later):]
    o_ref = refs[0]
    later_mod_ref = refs[1] if n_later else None
    qt_s, k_s, vt_s, gt_s, ogt_s, w_in_t_s, w_out_s = refs[-7:]
    tm = ATTN_HALF_TILE
    j = pl.program_id(1)

    if n_later:
        c_ref, b_ref = later[0], later[-1]

        @pl.when((pl.program_id(0) == 0) & (j == 0))
        def _():
            for l in range(n_later):
                later_mod_ref[l] = jnp.broadcast_to(b_ref[l], later_mod_ref.shape[1:])

        a = _silu(c_ref[0]).astype(jnp.bfloat16)
        for l in range(n_later):
            later_mod_ref[l] += jnp.dot(a, later[1 + l][0].astype(jnp.bfloat16),
                                        preferred_element_type=jnp.float32)

    @pl.when((pl.program_id(0) == 0) & (j == 0))
    def _():
        for c in range(ATTN_IN_WIDTH // LANES):
            cols = slice(c * LANES, (c + 1) * LANES)
            w_in_t_s[cols, :] = w_in_ref[0, :, cols].T.astype(jnp.bfloat16)
        for c in range(Q_WIDTH // LANES):
            rows = slice(c * LANES, (c + 1) * LANES)
            w_out_s[rows, :] = w_out_ref[0, rows, :].astype(jnp.bfloat16)

    @pl.when(j == 0)
    def _():
        k_s[0:BLOCK, :] = jnp.zeros((BLOCK, k_s.shape[1]), k_s.dtype)
        vt_s[:, 0:BLOCK] = jnp.zeros((vt_s.shape[0], BLOCK), vt_s.dtype)

    kk = lax.broadcasted_iota(jnp.int32, (BLOCK, 2 * BLOCK), 0)
    qq = lax.broadcasted_iota(jnp.int32, (BLOCK, 2 * BLOCK), 1) & (BLOCK - 1)
    use_cur = kk <= qq
    prev_bias = jnp.where(j > 0, 0.0, -jnp.inf)
    first_head = lax.broadcasted_iota(jnp.int32, (1, 2 * BLOCK), 1) < BLOCK
    zero = jnp.zeros((HEAD_DIM, BLOCK), jnp.bfloat16)
    zero_p = jnp.zeros((BLOCK, 2 * BLOCK), jnp.bfloat16)
    ones = jnp.ones((V_PAD, tm), jnp.bfloat16)
    n_pairs = N_HEADS // 2

    def projection(hf):
        toks = slice(hf * tm, (hf + 1) * tm)
        krows = slice(BLOCK + hf * tm, BLOCK + (hf + 1) * tm)
        st = {}

        def norm():
            st["h"] = _modulated_norm(x_ref[0, toks, :], mod_ref, g_ref)
            ang = pos_ref[0, :, toks].astype(jnp.float32) * invf_ref[...]
            st["cos"], st["sin"] = jnp.cos(ang), jnp.sin(ang)

        def norm_rope(t, gain):
            ssq = jnp.sum(t * t, axis=0, keepdims=True)
            tn = t * lax.rsqrt(ssq * (1.0 / HEAD_DIM) + NORM_EPS) * gain
            x1, x2 = tn[0:ROT_HALF], tn[ROT_HALF:ROT_DIM]
            cos, sin = st["cos"], st["sin"]
            return jnp.concatenate([x1 * cos - x2 * sin, x2 * cos + x1 * sin, tn[ROT_DIM:]], axis=0)

        def chunk(c):
            row0 = c * PROJ_CHUNK
            pt = lax.dot_general(w_in_t_s[row0:row0 + PROJ_CHUNK, :], st["h"], _NT,
                                 preferred_element_type=jnp.float32)
            if row0 >= Q_WIDTH + 2 * KV_WIDTH:
                g0 = row0 - (Q_WIDTH + 2 * KV_WIDTH)
                gt_s[g0:g0 + PROJ_CHUNK, toks] = _silu(pt)
                return
            for i in range(PROJ_CHUNK // HEAD_DIM):
                t = pt[i * HEAD_DIM:(i + 1) * HEAD_DIM]
                hd = row0 // HEAD_DIM + i
                if hd < N_HEADS:
                    qt_s[hd * HEAD_DIM:(hd + 1) * HEAD_DIM, toks] = (
                        norm_rope(t, qg_ref[...]).astype(jnp.bfloat16))
                elif hd < N_HEADS + N_KV_HEADS:
                    kh = hd - N_HEADS
                    kt = norm_rope(t, kg_ref[...])
                    k_s[krows, kh * LANES:(kh + 1) * LANES] = (
                        jnp.concatenate([kt, kt], axis=0).T.astype(jnp.bfloat16))
                else:
                    kh = hd - N_HEADS - N_KV_HEADS
                    vt_s[kh * V_ROWS:kh * V_ROWS + HEAD_DIM, krows] = t.astype(jnp.bfloat16)
                    vt_s[kh * V_ROWS + HEAD_DIM:(kh + 1) * V_ROWS, krows] = ones

        return [norm] + [(lambda c=c: chunk(c)) for c in range(ATTN_IN_WIDTH // PROJ_CHUNK)]

    def scores(qb):
        qcols = slice(qb * BLOCK, (qb + 1) * BLOCK)
        keys = slice(qb * BLOCK, (qb + 2) * BLOCK)
        out = []
        for pr in range(n_pairs):
            kh = pr // (GROUP // 2)
            kd = k_s[keys, kh * LANES:(kh + 1) * LANES]
            r0 = slice(2 * pr * HEAD_DIM, (2 * pr + 1) * HEAD_DIM)
            r1 = slice((2 * pr + 1) * HEAD_DIM, (2 * pr + 2) * HEAD_DIM)
            qz = jnp.concatenate([jnp.concatenate([qt_s[r0, qcols], zero], axis=1),
                                  jnp.concatenate([zero, qt_s[r1, qcols]], axis=1)], axis=0)
            s = jnp.dot(kd, qz, preferred_element_type=jnp.float32)
            s_prev = s[:BLOCK] + prev_bias if qb == 0 else s[:BLOCK]
            out.append(jnp.where(use_cur, s[BLOCK:], s_prev))
        return out

    def finish(qb, s_list):
        qcols = slice(qb * BLOCK, (qb + 1) * BLOCK)
        keys = slice(qb * BLOCK, (qb + 2) * BLOCK)
        for pr, s in enumerate(s_list):
            kh = pr // (GROUP // 2)
            vt = vt_s[kh * V_ROWS:(kh + 1) * V_ROWS, keys]
            r0 = slice(2 * pr * HEAD_DIM, (2 * pr + 1) * HEAD_DIM)
            r1 = slice((2 * pr + 1) * HEAD_DIM, (2 * pr + 2) * HEAD_DIM)
            sink = jnp.where(first_head, sink_ref[2 * pr], sink_ref[2 * pr + 1])
            m = jnp.maximum(jnp.max(s, axis=0, keepdims=True), sink)
            p = jnp.exp2(s - m).astype(jnp.bfloat16)
            p2 = jnp.concatenate([jnp.where(use_cur, zero_p, p), jnp.where(use_cur, p, zero_p)], axis=0)
            o = jnp.dot(vt, p2, preferred_element_type=jnp.float32)
            denom = o[HEAD_DIM:HEAD_DIM + 1] + jnp.exp2(sink - m)
            o = o[:HEAD_DIM] * (1.0 / denom)
            ogt_s[r0, qcols] = (o[:, :BLOCK] * gt_s[r0, qcols]).astype(jnp.bfloat16)
            ogt_s[r1, qcols] = (o[:, BLOCK:] * gt_s[r1, qcols]).astype(jnp.bfloat16)

    def attention(hf):
        blocks = range(hf * (tm // BLOCK), (hf + 1) * (tm // BLOCK))
        st = {}

        def first():
            st["s"] = scores(blocks[0])

        def step(qb):
            s_cur = st["s"]
            if qb + 1 in blocks:
                st["s"] = scores(qb + 1)
            finish(qb, s_cur)

        return [first] + [(lambda qb=qb: step(qb)) for qb in blocks]

    def out_projection(hf):
        toks = slice(hf * tm, (hf + 1) * tm)

        def run():
            out = lax.dot_general(ogt_s[:, toks], w_out_s[...], _TN,
                                  preferred_element_type=jnp.float32)
            o_ref[0, toks, :] = x_ref[0, toks, :] + mod_ref[0, 2:3, :] * out

        return [run]

    def interleave(a, b):
        done_a = done_b = 0
        while done_a < len(a) or done_b < len(b):
            if done_b >= len(b) or (done_a < len(a) and done_a * len(b) <= done_b * len(a)):
                a[done_a]()
                done_a += 1
            else:
                b[done_b]()
                done_b += 1

    proj0, proj1 = projection(0), projection(1)
    for f in proj0:
        f()
    proj1[0]()
    interleave(attention(0), proj1[1:])
    interleave(attention(1), out_projection(0))
    out_projection(1)[0]()

    step_tokens = 2 * tm
    k_s[0:BLOCK, :] = k_s[step_tokens:step_tokens + BLOCK, :]
    vt_s[:, 0:BLOCK] = vt_s[:, step_tokens:step_tokens + BLOCK]


def _attn_layer(x, mod, norm_g, positions, w_in_all, q_gain, k_gain, sinks, w_out_all, layer,
                later_adaln=None):
    b, s, d = x.shape
    tm = ATTN_HALF_TILE
    step = 2 * tm
    steps_per_seq = s // step
    inv_freq = ROPE_THETA ** (-jnp.arange(ROT_HALF, dtype=jnp.float32) * 2.0 / ROT_DIM)
    invf = jnp.broadcast_to(inv_freq[:, None], (ROT_HALF, tm))
    qg = jnp.broadcast_to((q_gain * (HEAD_DIM ** -0.5 * LOG2E))[:, None], (HEAD_DIM, tm))
    kg = jnp.broadcast_to(k_gain[:, None], (HEAD_DIM, tm))
    const = lambda i, j: (0, 0)
    out_specs = pl.BlockSpec((1, step, d), lambda i, j: (i, j, 0))
    out_shape = jax.ShapeDtypeStruct(x.shape, x.dtype)
    later_specs, later_args, n_later = [], [], 0
    if later_adaln is not None:
        c, ada_w, ada_b, first_layer = later_adaln
        n_later = ada_w.shape[0] - first_layer
        n_steps = b * steps_per_seq
        rows = d // n_steps
        assert rows * n_steps == d and rows % 8 == 0
        flat = lambda i, j: i * steps_per_seq + j
        later_specs = (
            [pl.BlockSpec((1, b, rows), lambda i, j: (flat(i, j), 0, 0))]
            + [pl.BlockSpec((1, rows, 3 * d), lambda i, j, l=l: (first_layer + l, flat(i, j), 0))
               for l in range(n_later)]
            + [pl.BlockSpec((n_later, 1, 3 * d), lambda i, j: (0, 0, 0))])
        later_args = ([c.reshape(b, n_steps, rows).transpose(1, 0, 2)] + [ada_w] * n_later
                      + [ada_b[first_layer:].reshape(n_later, 1, 3 * d)])
        out_specs = [out_specs, pl.BlockSpec((n_later, b, 3 * d), lambda i, j: (0, 0, 0))]
        out_shape = [out_shape, jax.ShapeDtypeStruct((n_later, b, 3 * d), jnp.float32)]
    return pl.pallas_call(
        functools.partial(_attn_kernel, n_later=n_later),
        grid=(b, steps_per_seq),
        in_specs=[
            pl.BlockSpec((1, step, d), lambda i, j: (i, j, 0)),
            pl.BlockSpec((1, 3, d), lambda i, j: (i, 0, 0)),
            pl.BlockSpec((1, d), const),
            pl.BlockSpec((1, 1, step), lambda i, j: (i, 0, j)),
            pl.BlockSpec((ROT_HALF, tm), const),
            pl.BlockSpec((1, d, ATTN_IN_WIDTH), lambda i, j: (layer, 0, 0)),
            pl.BlockSpec((1, Q_WIDTH, d), lambda i, j: (layer, 0, 0)),
            pl.BlockSpec((HEAD_DIM, tm), const),
            pl.BlockSpec((HEAD_DIM, tm), const),
            pl.BlockSpec(memory_space=pltpu.SMEM),
        ] + later_specs,
        out_specs=out_specs,
        out_shape=out_shape,
        scratch_shapes=[
            pltpu.VMEM((Q_WIDTH, step), jnp.bfloat16),
            pltpu.VMEM((BLOCK + step, N_KV_HEADS * LANES), jnp.bfloat16),
            pltpu.VMEM((N_KV_HEADS * V_ROWS, BLOCK + step), jnp.bfloat16),
            pltpu.VMEM((Q_WIDTH, step), jnp.float32),
            pltpu.VMEM((Q_WIDTH, step), jnp.bfloat16),
            pltpu.VMEM((ATTN_IN_WIDTH, d), jnp.bfloat16),
            pltpu.VMEM((Q_WIDTH, d), jnp.bfloat16),
        ],
        compiler_params=pltpu.CompilerParams(
            dimension_semantics=("arbitrary", "arbitrary"), vmem_limit_bytes=VMEM_LIMIT),
        name="attn_layer",
    )(x, mod, norm_g.reshape(1, d), positions.reshape(b, 1, s), invf,
      w_in_all, w_out_all, qg, kg, sinks * LOG2E, *later_args)


def _pool_kernel(x_ref, mod_ref, g_ref, w_in_ref, w_grp_ref, scale_ref, w_out_ref, o_ref,
                 v_s, w_in_s, w_grp_s, w_out_s):
    tm = POOL_HALF_TILE
    j = pl.program_id(1)

    @pl.when((pl.program_id(0) == 0) & (j == 0))
    def _():
        for c in range(D_MODEL // LANES):
            rows = slice(c * LANES, (c + 1) * LANES)
            w_in_s[rows, :] = w_in_ref[0, rows, :].astype(jnp.bfloat16)
            w_out_s[rows, :] = w_out_ref[0, rows, :].astype(jnp.bfloat16)
        for gi in range(len(POOL_WINDOWS)):
            w_grp_s[gi] = w_grp_ref[0, gi].astype(jnp.bfloat16)

    @pl.when(j == 0)
    def _():
        v_s[0:POOL_HALO, :] = jnp.zeros((POOL_HALO, v_s.shape[1]), v_s.dtype)

    gates = {}

    def projection(hf):
        toks = slice(hf * tm, (hf + 1) * tm)
        st = {}

        def norm():
            st["h"] = _modulated_norm(x_ref[0, toks, :], mod_ref, g_ref)

        def value_path():
            v_s[POOL_HALO + hf * tm:POOL_HALO + (hf + 1) * tm, :] = jnp.dot(
                st["h"], w_in_s[:, :D_MODEL], preferred_element_type=jnp.float32)

        def gate_path():
            gates[hf] = _silu(jnp.dot(st["h"], w_in_s[:, D_MODEL:],
                                      preferred_element_type=jnp.float32))

        return [norm, value_path, gate_path]

    def mixing(hf):
        toks = slice(hf * tm, (hf + 1) * tm)
        ext = slice(hf * tm, POOL_HALO + (hf + 1) * tm)
        t = lax.broadcasted_iota(jnp.int32, (tm, POOL_GROUP_DIM), 0) + (j * 2 + hf) * tm
        mixed = []

        def group(gi):
            w = POOL_WINDOWS[gi]
            cols = slice(gi * POOL_GROUP_DIM, (gi + 1) * POOL_GROUP_DIM)
            win = v_s[ext, cols]
            cur = win[POOL_HALO:]
            span = 1
            while span < w:
                win = win + pltpu.roll(win, span, 0)
                span *= 2
            count = jnp.minimum(t + 1, w).astype(jnp.float32)
            pooled = win[POOL_HALO:] / count - cur
            mixed.append(jnp.dot(pooled.astype(jnp.bfloat16), w_grp_s[gi],
                                 preferred_element_type=jnp.float32))

        def out():
            m = jnp.concatenate(mixed, axis=1) * scale_ref[...]
            o = jnp.dot((m * gates[hf]).astype(jnp.bfloat16), w_out_s[...],
                        preferred_element_type=jnp.float32)
            o_ref[0, toks, :] = x_ref[0, toks, :] + mod_ref[0, 2:3, :] * o

        return [(lambda gi=gi: group(gi)) for gi in range(len(POOL_WINDOWS))] + [out]

    def interleave(a, b):
        done_a = done_b = 0
        while done_a < len(a) or done_b < len(b):
            if done_b >= len(b) or (done_a < len(a) and done_a * len(b) <= done_b * len(a)):
                a[done_a]()
                done_a += 1
            else:
                b[done_b]()
                done_b += 1

    proj0, proj1 = projection(0), projection(1)
    for f in proj0:
        f()
    proj1[0]()
    interleave(mixing(0), proj1[1:])
    for f in mixing(1):
        f()

    v_s[0:POOL_HALO, :] = v_s[2 * tm:2 * tm + POOL_HALO, :]


def _pool_layer(x, mod, norm_g, w_in_all, w_group_all, scale, w_out_all, layer):
    b, s, d = x.shape
    tm = 2 * POOL_HALF_TILE
    n_grp = len(POOL_WINDOWS)
    const = lambda i, j: (0, 0)
    return pl.pallas_call(
        _pool_kernel,
        grid=(b, s // tm),
        in_specs=[
            pl.BlockSpec((1, tm, d), lambda i, j: (i, j, 0)),
            pl.BlockSpec((1, 3, d), lambda i, j: (i, 0, 0)),
            pl.BlockSpec((1, d), const),
            pl.BlockSpec((1, d, 2 * D_MODEL), lambda i, j: (layer, 0, 0)),
            pl.BlockSpec((1, n_grp, POOL_GROUP_DIM, POOL_GROUP_DIM), lambda i, j: (layer, 0, 0, 0)),
            pl.BlockSpec((1, d), const),
            pl.BlockSpec((1, D_MODEL, d), lambda i, j: (layer, 0, 0)),
        ],
        out_specs=pl.BlockSpec((1, tm, d), lambda i, j: (i, j, 0)),
        out_shape=jax.ShapeDtypeStruct(x.shape, x.dtype),
        scratch_shapes=[
            pltpu.VMEM((POOL_HALO + tm, D_MODEL), jnp.float32),
            pltpu.VMEM((d, 2 * D_MODEL), jnp.bfloat16),
            pltpu.VMEM((n_grp, POOL_GROUP_DIM, POOL_GROUP_DIM), jnp.bfloat16),
            pltpu.VMEM((D_MODEL, d), jnp.bfloat16),
        ],
        compiler_params=pltpu.CompilerParams(
            dimension_semantics=("arbitrary", "arbitrary"), vmem_limit_bytes=VMEM_LIMIT),
        name="pool_layer",
    )(x, mod, norm_g.reshape(1, d), w_in_all, w_group_all, scale.reshape(1, d), w_out_all)


def kernel(x, c, positions, ada_w, ada_b, norm_g, attn_w_in, attn_q_norm, attn_k_norm, attn_sinks,
           attn_w_out, pool_w_in, pool_w_group, pool_scale, pool_w_out):
    depth = ada_w.shape[0]
    b = x.shape[0]
    mods = [_adaln(c, ada_w, ada_b, 1)[0].reshape(b, 3, D_MODEL)]
    for i in range(depth):
        jl = i // 2
        if i % 2 == 0:
            later = (c, ada_w, ada_b, 1) if (i == 0 and depth > 1) else None
            x = _attn_layer(x, mods[i], norm_g[i], positions, attn_w_in, attn_q_norm[jl],
                            attn_k_norm[jl], attn_sinks[jl], attn_w_out, jl, later)
            if later is not None:
                x, later_mods = x
                mods += [later_mods[l].reshape(b, 3, D_MODEL) for l in range(depth - 1)]
        else:
            x = _pool_layer(x, mods[i], norm_g[i], pool_w_in, pool_w_group,
                            pool_scale[jl], pool_w_out, jl)
    return x
```

```python
import functools
import math

import jax
import jax.numpy as jnp
from jax import lax
from jax.experimental import pallas as pl
from jax.experimental.pallas import tpu as pltpu

D_MODEL = 1024
HEAD_DIM = 64
N_HEADS = 16
N_KV_HEADS = 4
GROUP = N_HEADS // N_KV_HEADS
Q_WIDTH = N_HEADS * HEAD_DIM
KV_WIDTH = N_KV_HEADS * HEAD_DIM
ATTN_IN_WIDTH = 2 * Q_WIDTH + 2 * KV_WIDTH
BLOCK = 128
ROT_DIM = HEAD_DIM // 4
ROT_HALF = ROT_DIM // 2
ROPE_THETA = 500000.0
POOL_WINDOWS = (2, 4, 8, 16)
POOL_GROUP_DIM = D_MODEL // len(POOL_WINDOWS)
POOL_HALO = 16
NORM_EPS = 1e-6
LANES = 128
BF16_ROWS = 16
V_PAD = BF16_ROWS
V_ROWS = HEAD_DIM + V_PAD
LOG2E = math.log2(math.e)

ATTN_HALF_TILE = 512
PROJ_CHUNK = 256
POOL_HALF_TILE = 512
VMEM_LIMIT = 56 * 1024 * 1024

_NT = (((1,), (1,)), ((), ()))
_TN = (((0,), (0,)), ((), ()))


def _silu(x):
    hx = 0.5 * x
    return hx + hx * jnp.tanh(hx)


def _adaln_kernel(c_ref, w_ref, b_ref, o_ref):
    a = _silu(c_ref[...]).astype(jnp.bfloat16)
    w = w_ref[0].astype(jnp.bfloat16)
    o_ref[0] = jnp.dot(a, w, preferred_element_type=jnp.float32) + b_ref[0]


def _adaln(c, ada_w, ada_b, depth):
    _, d, n = ada_w.shape
    b = c.shape[0]
    tn = 1024
    return pl.pallas_call(
        _adaln_kernel,
        grid=(depth, n // tn),
        in_specs=[
            pl.BlockSpec((b, d), lambda l, j: (0, 0)),
            pl.BlockSpec((1, d, tn), lambda l, j: (l, 0, j)),
            pl.BlockSpec((1, 1, tn), lambda l, j: (l, 0, j)),
        ],
        out_specs=pl.BlockSpec((1, b, tn), lambda l, j: (l, 0, j)),
        out_shape=jax.ShapeDtypeStruct((depth, b, n), jnp.float32),
        compiler_params=pltpu.CompilerParams(
            dimension_semantics=("arbitrary", "arbitrary"), vmem_limit_bytes=VMEM_LIMIT),
        name="adaln_mod",
    )(c, ada_w, ada_b.reshape(ada_b.shape[0], 1, n))


def _modulation(mod_ref, batch):
    row = mod_ref[0, pl.ds(batch, 1), :]
    return row[:, :D_MODEL], row[:, D_MODEL:2 * D_MODEL], row[:, 2 * D_MODEL:]


def _modulated_norm(x, shift, scale, g):
    ms = jnp.mean(x * x, axis=-1, keepdims=True)
    gain = g * (1.0 + scale)
    h = x * lax.rsqrt(ms + NORM_EPS) * gain + shift
    return h.astype(jnp.bfloat16)


def _attn_kernel(*refs, n_later, layer, attn_layer):
    n_in = 8
    x_ref, mod_ref, g_ref, pos_ref, small_ref, w_in_ref, w_out_ref, sink_ref = refs[:n_in]
    later = refs[n_in:n_in + (n_later + 2 if n_later else 0)]
    refs = refs[n_in + len(later):]
    o_ref = refs[0]
    later_mod_ref = refs[1] if n_later else None
    qt_s, k_s, vt_s, gt_s, ogt_s, w_in_t_s, w_out_s = refs[-7:]
    tm = ATTN_HALF_TILE
    batch = pl.program_id(0)
    j = pl.program_id(1)

    shift, scale, gate = _modulation(mod_ref, batch)
    g_norm = g_ref[layer:layer + 1, :]
    lane_tile = lambda a: jnp.concatenate([a] * (tm // LANES), axis=1)
    invf = lane_tile(small_ref[0:ROT_HALF, :])
    g0 = ROT_HALF + attn_layer * 2 * HEAD_DIM
    q_gain = lane_tile(small_ref[g0:g0 + HEAD_DIM, :])
    k_gain = lane_tile(small_ref[g0 + HEAD_DIM:g0 + 2 * HEAD_DIM, :])

    if n_later:
        c_ref, b_ref = later[0], later[-1]

        @pl.when((pl.program_id(0) == 0) & (j == 0))
        def _():
            for l in range(n_later):
                later_mod_ref[l] = jnp.broadcast_to(b_ref[layer + 1 + l:layer + 2 + l, :],
                                                    later_mod_ref.shape[1:])

        a = _silu(c_ref[0]).astype(jnp.bfloat16)
        for l in range(n_later):
            later_mod_ref[l] += jnp.dot(a, later[1 + l][0].astype(jnp.bfloat16),
                                        preferred_element_type=jnp.float32)

    @pl.when((pl.program_id(0) == 0) & (j == 0))
    def _():
        for c in range(ATTN_IN_WIDTH // LANES):
            cols = slice(c * LANES, (c + 1) * LANES)
            w_in_t_s[cols, :] = w_in_ref[0, :, cols].T.astype(jnp.bfloat16)
        for c in range(Q_WIDTH // LANES):
            rows = slice(c * LANES, (c + 1) * LANES)
            w_out_s[rows, :] = w_out_ref[0, rows, :].astype(jnp.bfloat16)

    @pl.when(j == 0)
    def _():
        k_s[0:BLOCK, :] = jnp.zeros((BLOCK, k_s.shape[1]), k_s.dtype)
        vt_s[:, 0:BLOCK] = jnp.zeros((vt_s.shape[0], BLOCK), vt_s.dtype)

    kk = lax.broadcasted_iota(jnp.int32, (BLOCK, 2 * BLOCK), 0)
    qq = lax.broadcasted_iota(jnp.int32, (BLOCK, 2 * BLOCK), 1) & (BLOCK - 1)
    use_cur = kk <= qq
    prev_bias = jnp.where(j > 0, 0.0, -jnp.inf)
    first_head = lax.broadcasted_iota(jnp.int32, (1, 2 * BLOCK), 1) < BLOCK
    zero = jnp.zeros((HEAD_DIM, BLOCK), jnp.bfloat16)
    zero_p = jnp.zeros((BLOCK, 2 * BLOCK), jnp.bfloat16)
    ones = jnp.ones((V_PAD, tm), jnp.bfloat16)
    n_pairs = N_HEADS // 2

    def projection(hf):
        toks = slice(hf * tm, (hf + 1) * tm)
        krows = slice(BLOCK + hf * tm, BLOCK + (hf + 1) * tm)
        st = {}

        def norm():
            st["h"] = _modulated_norm(x_ref[0, toks, :], shift, scale, g_norm)
            ang = pos_ref[pl.ds(batch, 1), toks].astype(jnp.float32) * invf
            st["cos"], st["sin"] = jnp.cos(ang), jnp.sin(ang)

        def norm_rope(t, gain):
            ssq = jnp.sum(t * t, axis=0, keepdims=True)
            tn = t * lax.rsqrt(ssq * (1.0 / HEAD_DIM) + NORM_EPS) * gain
            x1, x2 = tn[0:ROT_HALF], tn[ROT_HALF:ROT_DIM]
            cos, sin = st["cos"], st["sin"]
            return jnp.concatenate([x1 * cos - x2 * sin, x2 * cos + x1 * sin, tn[ROT_DIM:]], axis=0)

        def chunk(c):
            row0 = c * PROJ_CHUNK
            pt = lax.dot_general(w_in_t_s[row0:row0 + PROJ_CHUNK, :], st["h"], _NT,
                                 preferred_element_type=jnp.float32)
            if row0 >= Q_WIDTH + 2 * KV_WIDTH:
                g0 = row0 - (Q_WIDTH + 2 * KV_WIDTH)
                gt_s[g0:g0 + PROJ_CHUNK, toks] = _silu(pt)
                return
            for i in range(PROJ_CHUNK // HEAD_DIM):
                t = pt[i * HEAD_DIM:(i + 1) * HEAD_DIM]
                hd = row0 // HEAD_DIM + i
                if hd < N_HEADS:
                    qt_s[hd * HEAD_DIM:(hd + 1) * HEAD_DIM, toks] = (
                        norm_rope(t, q_gain).astype(jnp.bfloat16))
                elif hd < N_HEADS + N_KV_HEADS:
                    kh = hd - N_HEADS
                    kt = norm_rope(t, k_gain)
                    k_s[krows, kh * LANES:(kh + 1) * LANES] = (
                        jnp.concatenate([kt, kt], axis=0).T.astype(jnp.bfloat16))
                else:
                    kh = hd - N_HEADS - N_KV_HEADS
                    vt_s[kh * V_ROWS:kh * V_ROWS + HEAD_DIM, krows] = t.astype(jnp.bfloat16)
                    vt_s[kh * V_ROWS + HEAD_DIM:(kh + 1) * V_ROWS, krows] = ones

        return [norm] + [(lambda c=c: chunk(c)) for c in range(ATTN_IN_WIDTH // PROJ_CHUNK)]

    def scores(qb):
        qcols = slice(qb * BLOCK, (qb + 1) * BLOCK)
        keys = slice(qb * BLOCK, (qb + 2) * BLOCK)
        out = []
        for pr in range(n_pairs):
            kh = pr // (GROUP // 2)
            kd = k_s[keys, kh * LANES:(kh + 1) * LANES]
            r0 = slice(2 * pr * HEAD_DIM, (2 * pr + 1) * HEAD_DIM)
            r1 = slice((2 * pr + 1) * HEAD_DIM, (2 * pr + 2) * HEAD_DIM)
            qz = jnp.concatenate([jnp.concatenate([qt_s[r0, qcols], zero], axis=1),
                                  jnp.concatenate([zero, qt_s[r1, qcols]], axis=1)], axis=0)
            s = jnp.dot(kd, qz, preferred_element_type=jnp.float32)
            s_prev = s[:BLOCK] + prev_bias if qb == 0 else s[:BLOCK]
            out.append(jnp.where(use_cur, s[BLOCK:], s_prev))
        return out

    def finish(qb, s_list):
        qcols = slice(qb * BLOCK, (qb + 1) * BLOCK)
        keys = slice(qb * BLOCK, (qb + 2) * BLOCK)
        for pr, s in enumerate(s_list):
            kh = pr // (GROUP // 2)
            vt = vt_s[kh * V_ROWS:(kh + 1) * V_ROWS, keys]
            r0 = slice(2 * pr * HEAD_DIM, (2 * pr + 1) * HEAD_DIM)
            r1 = slice((2 * pr + 1) * HEAD_DIM, (2 * pr + 2) * HEAD_DIM)
            sink = jnp.where(first_head, sink_ref[attn_layer, 2 * pr],
                             sink_ref[attn_layer, 2 * pr + 1]) * LOG2E
            m = jnp.maximum(jnp.max(s, axis=0, keepdims=True), sink)
            p = jnp.exp2(s - m).astype(jnp.bfloat16)
            p2 = jnp.concatenate([jnp.where(use_cur, zero_p, p), jnp.where(use_cur, p, zero_p)], axis=0)
            o = jnp.dot(vt, p2, preferred_element_type=jnp.float32)
            denom = o[HEAD_DIM:HEAD_DIM + 1] + jnp.exp2(sink - m)
            o = o[:HEAD_DIM] * (1.0 / denom)
            ogt_s[r0, qcols] = (o[:, :BLOCK] * gt_s[r0, qcols]).astype(jnp.bfloat16)
            ogt_s[r1, qcols] = (o[:, BLOCK:] * gt_s[r1, qcols]).astype(jnp.bfloat16)

    def attention(hf):
        blocks = range(hf * (tm // BLOCK), (hf + 1) * (tm // BLOCK))
        st = {}

        def first():
            st["s"] = scores(blocks[0])

        def step(qb):
            s_cur = st["s"]
            if qb + 1 in blocks:
                st["s"] = scores(qb + 1)
            finish(qb, s_cur)

        return [first] + [(lambda qb=qb: step(qb)) for qb in blocks]

    def out_projection(hf):
        toks = slice(hf * tm, (hf + 1) * tm)

        def run():
            out = lax.dot_general(ogt_s[:, toks], w_out_s[...], _TN,
                                  preferred_element_type=jnp.float32)
            o_ref[0, toks, :] = x_ref[0, toks, :] + gate * out

        return [run]

    def interleave(a, b):
        done_a = done_b = 0
        while done_a < len(a) or done_b < len(b):
            if done_b >= len(b) or (done_a < len(a) and done_a * len(b) <= done_b * len(a)):
                a[done_a]()
                done_a += 1
            else:
                b[done_b]()
                done_b += 1

    proj0, proj1 = projection(0), projection(1)
    for f in proj0:
        f()
    proj1[0]()
    interleave(attention(0), proj1[1:])
    interleave(attention(1), out_projection(0))
    out_projection(1)[0]()

    step_tokens = 2 * tm
    k_s[0:BLOCK, :] = k_s[step_tokens:step_tokens + BLOCK, :]
    vt_s[:, 0:BLOCK] = vt_s[:, step_tokens:step_tokens + BLOCK]


def _attn_small_params(q_norm, k_norm):
    inv_freq = ROPE_THETA ** (-jnp.arange(ROT_HALF, dtype=jnp.float32) * 2.0 / ROT_DIM)
    cols = [inv_freq]
    for l in range(q_norm.shape[0]):
        cols += [q_norm[l] * (HEAD_DIM ** -0.5 * LOG2E), k_norm[l]]
    col = jnp.concatenate(cols)
    return jnp.broadcast_to(col[:, None], (col.shape[0], LANES))


def _attn_layer(x, mod_all, mod_idx, norm_g, positions, small, w_in_all, sinks, w_out_all,
                layer, attn_layer, later_adaln=None):
    b, s, d = x.shape
    tm = ATTN_HALF_TILE
    step = 2 * tm
    steps_per_seq = s // step
    const = lambda i, j: (0, 0)
    out_specs = pl.BlockSpec((1, step, d), lambda i, j: (i, j, 0))
    out_shape = jax.ShapeDtypeStruct(x.shape, x.dtype)
    later_specs, later_args, n_later = [], [], 0
    if later_adaln is not None:
        c, ada_w, ada_b, first_layer = later_adaln
        assert first_layer == layer + 1
        n_later = ada_w.shape[0] - first_layer
        n_steps = b * steps_per_seq
        rows = d // n_steps
        assert rows * n_steps == d and rows % 8 == 0
        flat = lambda i, j: i * steps_per_seq + j
        later_specs = (
            [pl.BlockSpec((1, b, rows), lambda i, j: (flat(i, j), 0, 0))]
            + [pl.BlockSpec((1, rows, 3 * d), lambda i, j, l=l: (first_layer + l, flat(i, j), 0))
               for l in range(n_later)]
            + [pl.BlockSpec(ada_b.shape, const)])
        later_args = [c.reshape(b, n_steps, rows).transpose(1, 0, 2)] + [ada_w] * n_later + [ada_b]
        out_specs = [out_specs, pl.BlockSpec((n_later, b, 3 * d), lambda i, j: (0, 0, 0))]
        out_shape = [out_shape, jax.ShapeDtypeStruct((n_later, b, 3 * d), jnp.float32)]
    return pl.pallas_call(
        functools.partial(_attn_kernel, n_later=n_later, layer=layer, attn_layer=attn_layer),
        grid=(b, steps_per_seq),
        in_specs=[
            pl.BlockSpec((1, step, d), lambda i, j: (i, j, 0)),
            pl.BlockSpec((1, b, 3 * d), lambda i, j: (mod_idx, 0, 0)),
            pl.BlockSpec(norm_g.shape, const),
            pl.BlockSpec((b, step), lambda i, j: (0, j)),
            pl.BlockSpec(small.shape, const),
            pl.BlockSpec((1, d, ATTN_IN_WIDTH), lambda i, j: (attn_layer, 0, 0)),
            pl.BlockSpec((1, Q_WIDTH, d), lambda i, j: (attn_layer, 0, 0)),
            pl.BlockSpec(memory_space=pltpu.SMEM),
        ] + later_specs,
        out_specs=out_specs,
        out_shape=out_shape,
        scratch_shapes=[
            pltpu.VMEM((Q_WIDTH, step), jnp.bfloat16),
            pltpu.VMEM((BLOCK + step, N_KV_HEADS * LANES), jnp.bfloat16),
            pltpu.VMEM((N_KV_HEADS * V_ROWS, BLOCK + step), jnp.bfloat16),
            pltpu.VMEM((Q_WIDTH, step), jnp.float32),
            pltpu.VMEM((Q_WIDTH, step), jnp.bfloat16),
            pltpu.VMEM((ATTN_IN_WIDTH, d), jnp.bfloat16),
            pltpu.VMEM((Q_WIDTH, d), jnp.bfloat16),
        ],
        compiler_params=pltpu.CompilerParams(
            dimension_semantics=("arbitrary", "arbitrary"), vmem_limit_bytes=VMEM_LIMIT),
        name="attn_layer",
    )(x, mod_all, norm_g, positions, small, w_in_all, w_out_all, sinks, *later_args)


def _pool_kernel(x_ref, mod_ref, g_ref, w_in_ref, w_grp_ref, scale_ref, w_out_ref, o_ref,
                 v_s, w_in_s, w_grp_s, w_out_s, *, layer, pool_layer):
    tm = POOL_HALF_TILE
    j = pl.program_id(1)
    shift, scale, gate = _modulation(mod_ref, pl.program_id(0))
    g_norm = g_ref[layer:layer + 1, :]
    mix_scale = scale_ref[pool_layer:pool_layer + 1, :]

    @pl.when((pl.program_id(0) == 0) & (j == 0))
    def _():
        for c in range(D_MODEL // LANES):
            rows = slice(c * LANES, (c + 1) * LANES)
            w_in_s[rows, :] = w_in_ref[0, rows, :].astype(jnp.bfloat16)
            w_out_s[rows, :] = w_out_ref[0, rows, :].astype(jnp.bfloat16)
        for gi in range(len(POOL_WINDOWS)):
            w_grp_s[gi] = w_grp_ref[0, gi].astype(jnp.bfloat16)

    @pl.when(j == 0)
    def _():
        v_s[0:POOL_HALO, :] = jnp.zeros((POOL_HALO, v_s.shape[1]), v_s.dtype)

    gates = {}

    def projection(hf):
        toks = slice(hf * tm, (hf + 1) * tm)
        st = {}

        def norm():
            st["h"] = _modulated_norm(x_ref[0, toks, :], shift, scale, g_norm)

        def value_path():
            v_s[POOL_HALO + hf * tm:POOL_HALO + (hf + 1) * tm, :] = jnp.dot(
                st["h"], w_in_s[:, :D_MODEL], preferred_element_type=jnp.float32)

        def gate_path():
            gates[hf] = _silu(jnp.dot(st["h"], w_in_s[:, D_MODEL:],
                                      preferred_element_type=jnp.float32))

        return [norm, value_path, gate_path]

    def mixing(hf):
        toks = slice(hf * tm, (hf + 1) * tm)
        ext = slice(hf * tm, POOL_HALO + (hf + 1) * tm)
        t = lax.broadcasted_iota(jnp.int32, (tm, POOL_GROUP_DIM), 0) + (j * 2 + hf) * tm
        mixed = []

        def group(gi):
            w = POOL_WINDOWS[gi]
            cols = slice(gi * POOL_GROUP_DIM, (gi + 1) * POOL_GROUP_DIM)
            win = v_s[ext, cols]
            cur = win[POOL_HALO:]
            span = 1
            while span < w:
                win = win + pltpu.roll(win, span, 0)
                span *= 2
            count = jnp.minimum(t + 1, w).astype(jnp.float32)
            pooled = win[POOL_HALO:] / count - cur
            mixed.append(jnp.dot(pooled.astype(jnp.bfloat16), w_grp_s[gi],
                                 preferred_element_type=jnp.float32))

        def out():
            m = jnp.concatenate(mixed, axis=1) * mix_scale
            o = jnp.dot((m * gates[hf]).astype(jnp.bfloat16), w_out_s[...],
                        preferred_element_type=jnp.float32)
            o_ref[0, toks, :] = x_ref[0, toks, :] + gate * o

        return [(lambda gi=gi: group(gi)) for gi in range(len(POOL_WINDOWS))] + [out]

    def interleave(a, b):
        done_a = done_b = 0
        while done_a < len(a) or done_b < len(b):
            if done_b >= len(b) or (done_a < len(a) and done_a * len(b) <= done_b * len(a)):
                a[done_a]()
                done_a += 1
            else:
                b[done_b]()
                done_b += 1

    proj0, proj1 = projection(0), projection(1)
    for f in proj0:
        f()
    proj1[0]()
    interleave(mixing(0), proj1[1:])
    for f in mixing(1):
        f()

    v_s[0:POOL_HALO, :] = v_s[2 * tm:2 * tm + POOL_HALO, :]


def _pool_layer(x, mod_all, mod_idx, norm_g, w_in_all, w_group_all, scale_all, w_out_all,
                layer, pool_layer):
    b, s, d = x.shape
    tm = 2 * POOL_HALF_TILE
    n_grp = len(POOL_WINDOWS)
    const = lambda i, j: (0, 0)
    return pl.pallas_call(
        functools.partial(_pool_kernel, layer=layer, pool_layer=pool_layer),
        grid=(b, s // tm),
        in_specs=[
            pl.BlockSpec((1, tm, d), lambda i, j: (i, j, 0)),
            pl.BlockSpec((1, b, 3 * d), lambda i, j: (mod_idx, 0, 0)),
            pl.BlockSpec(norm_g.shape, const),
            pl.BlockSpec((1, d, 2 * D_MODEL), lambda i, j: (pool_layer, 0, 0)),
            pl.BlockSpec((1, n_grp, POOL_GROUP_DIM, POOL_GROUP_DIM), lambda i, j: (pool_layer, 0, 0, 0)),
            pl.BlockSpec(scale_all.shape, const),
            pl.BlockSpec((1, D_MODEL, d), lambda i, j: (pool_layer, 0, 0)),
        ],
        out_specs=pl.BlockSpec((1, tm, d), lambda i, j: (i, j, 0)),
        out_shape=jax.ShapeDtypeStruct(x.shape, x.dtype),
        scratch_shapes=[
            pltpu.VMEM((POOL_HALO + tm, D_MODEL), jnp.float32),
            pltpu.VMEM((d, 2 * D_MODEL), jnp.bfloat16),
            pltpu.VMEM((n_grp, POOL_GROUP_DIM, POOL_GROUP_DIM), jnp.bfloat16),
            pltpu.VMEM((D_MODEL, d), jnp.bfloat16),
        ],
        compiler_params=pltpu.CompilerParams(
            dimension_semantics=("arbitrary", "arbitrary"), vmem_limit_bytes=VMEM_LIMIT),
        name="pool_layer",
    )(x, mod_all, norm_g, w_in_all, w_group_all, scale_all, w_out_all)


def kernel(x, c, positions, ada_w, ada_b, norm_g, attn_w_in, attn_q_norm, attn_k_norm, attn_sinks,
           attn_w_out, pool_w_in, pool_w_group, pool_scale, pool_w_out):
    depth = ada_w.shape[0]
    small = _attn_small_params(attn_q_norm, attn_k_norm)
    mod0 = _adaln(c, ada_w, ada_b, 1)
    mods = [(mod0, 0)]
    for i in range(depth):
        jl = i // 2
        mod_all, mod_idx = mods[i]
        if i % 2 == 0:
            later = (c, ada_w, ada_b, 1) if (i == 0 and depth > 1) else None
            x = _attn_layer(x, mod_all, mod_idx, norm_g, positions, small, attn_w_in, attn_sinks,
                            attn_w_out, i, jl, later)
            if later is not None:
                x, later_mods = x
                mods += [(later_mods, l) for l in range(depth - 1)]
        else:
            x = _pool_layer(x, mod_all, mod_idx, norm_g, pool_w_in, pool_w_group, pool_scale,
                            pool_w_out, i, jl)
    return x
```

```python
import functools
import math

import jax
import jax.numpy as jnp
from jax import lax
from jax.experimental import pallas as pl
from jax.experimental.pallas import tpu as pltpu

D_MODEL = 1024
HEAD_DIM = 64
N_HEADS = 16
N_KV_HEADS = 4
GROUP = N_HEADS // N_KV_HEADS
Q_WIDTH = N_HEADS * HEAD_DIM
KV_WIDTH = N_KV_HEADS * HEAD_DIM
ATTN_IN_WIDTH = 2 * Q_WIDTH + 2 * KV_WIDTH
BLOCK = 128
SUB = BLOCK // 2
ROT_DIM = HEAD_DIM // 4
ROT_HALF = ROT_DIM // 2
ROPE_THETA = 500000.0
POOL_WINDOWS = (2, 4, 8, 16)
POOL_GROUP_DIM = D_MODEL // len(POOL_WINDOWS)
POOL_HALO = 16
NORM_EPS = 1e-6
LANES = 128
BF16_ROWS = 16
V_PAD = BF16_ROWS
V_ROWS = HEAD_DIM + V_PAD
LOG2E = math.log2(math.e)

ATTN_HALF_TILE = 512
PROJ_CHUNK = 256
POOL_HALF_TILE = 512
VMEM_LIMIT = 60 * 1024 * 1024

_NT = (((1,), (1,)), ((), ()))
_TN = (((0,), (0,)), ((), ()))


def _silu(x):
    hx = 0.5 * x
    return hx + hx * jnp.tanh(hx)


def _adaln_kernel(c_ref, w_ref, b_ref, o_ref):
    a = _silu(c_ref[...]).astype(jnp.bfloat16)
    w = w_ref[0].astype(jnp.bfloat16)
    bias = b_ref[pl.ds(pl.program_id(0), 1), :]
    o_ref[0] = jnp.dot(a, w, preferred_element_type=jnp.float32) + bias


def _adaln(c, ada_w, ada_b, depth):
    _, d, n = ada_w.shape
    b = c.shape[0]
    tn = 1024
    return pl.pallas_call(
        _adaln_kernel,
        grid=(depth, n // tn),
        in_specs=[
            pl.BlockSpec((b, d), lambda l, j: (0, 0)),
            pl.BlockSpec((1, d, tn), lambda l, j: (l, 0, j)),
            pl.BlockSpec((ada_b.shape[0], tn), lambda l, j: (0, j)),
        ],
        out_specs=pl.BlockSpec((1, b, tn), lambda l, j: (l, 0, j)),
        out_shape=jax.ShapeDtypeStruct((depth, b, n), jnp.float32),
        compiler_params=pltpu.CompilerParams(
            dimension_semantics=("arbitrary", "arbitrary"), vmem_limit_bytes=VMEM_LIMIT),
        name="adaln_mod",
    )(c, ada_w, ada_b)


def _modulation(mod_ref, batch):
    row = mod_ref[0, pl.ds(batch, 1), :]
    return row[:, :D_MODEL], row[:, D_MODEL:2 * D_MODEL], row[:, 2 * D_MODEL:]


def _modulated_norm(x, shift, scale, g):
    ms = jnp.mean(x * x, axis=-1, keepdims=True)
    gain = g * (1.0 + scale)
    h = x * lax.rsqrt(ms + NORM_EPS) * gain + shift
    return h.astype(jnp.bfloat16)


def _attn_kernel(*refs, n_later, layer, attn_layer):
    n_in = 8
    x_ref, mod_ref, g_ref, pos_ref, small_ref, w_in_ref, w_out_ref, sink_ref = refs[:n_in]
    later = refs[n_in:n_in + (n_later + 2 if n_later else 0)]
    refs = refs[n_in + len(later):]
    o_ref = refs[0]
    later_mod_ref = refs[1] if n_later else None
    qa_s, qb_s, k_s, vt_s, gt_s, ogt_s, w_in_t_s, w_out_s = refs[-8:]
    tm = ATTN_HALF_TILE
    batch = pl.program_id(0)
    j = pl.program_id(1)

    shift, scale, gate = _modulation(mod_ref, batch)
    g_norm = g_ref[layer:layer + 1, :]
    lane_tile = lambda a: jnp.concatenate([a] * (tm // LANES), axis=1)
    invf = lane_tile(small_ref[0:ROT_HALF, :])
    g0 = ROT_HALF + attn_layer * 2 * HEAD_DIM
    q_gain = lane_tile(small_ref[g0:g0 + HEAD_DIM, :])
    k_gain = lane_tile(small_ref[g0 + HEAD_DIM:g0 + 2 * HEAD_DIM, :])

    if n_later:
        c_ref, b_ref = later[0], later[-1]

        @pl.when((pl.program_id(0) == 0) & (j == 0))
        def _():
            for l in range(n_later):
                later_mod_ref[l] = jnp.broadcast_to(b_ref[layer + 1 + l:layer + 2 + l, :],
                                                    later_mod_ref.shape[1:])

        a = _silu(c_ref[0]).astype(jnp.bfloat16)
        for l in range(n_later):
            later_mod_ref[l] += jnp.dot(a, later[1 + l][0].astype(jnp.bfloat16),
                                        preferred_element_type=jnp.float32)

    @pl.when((pl.program_id(0) == 0) & (j == 0))
    def _():
        for c in range(ATTN_IN_WIDTH // LANES):
            cols = slice(c * LANES, (c + 1) * LANES)
            w_in_t_s[cols, :] = w_in_ref[0, :, cols].T.astype(jnp.bfloat16)
        for c in range(Q_WIDTH // LANES):
            rows = slice(c * LANES, (c + 1) * LANES)
            w_out_s[rows, :] = w_out_ref[0, rows, :].astype(jnp.bfloat16)

    @pl.when(j == 0)
    def _():
        k_s[0:BLOCK, :] = jnp.zeros((BLOCK, k_s.shape[1]), k_s.dtype)
        vt_s[:, 0:BLOCK] = jnp.zeros((vt_s.shape[0], BLOCK), vt_s.dtype)

    kq = lax.broadcasted_iota(jnp.int32, (SUB, 2 * BLOCK), 0)
    iq = lax.broadcasted_iota(jnp.int32, (SUB, 2 * BLOCK), 1) & (SUB - 1)
    use_new = kq <= iq
    prev_bias = jnp.where(j > 0, 0.0, -jnp.inf)
    lane4 = lax.broadcasted_iota(jnp.int32, (1, 2 * BLOCK), 1) // SUB
    low_q = (lax.broadcasted_iota(jnp.int32, (HEAD_DIM, tm), 1) & SUB) == 0
    low_o = lax.broadcasted_iota(jnp.int32, (HEAD_DIM, LANES), 1) < SUB
    zero = jnp.zeros((HEAD_DIM, LANES), jnp.bfloat16)
    zero_p = jnp.zeros((SUB, 2 * BLOCK), jnp.bfloat16)
    ones = jnp.ones((V_PAD, tm), jnp.bfloat16)

    def projection(hf):
        toks = slice(hf * tm, (hf + 1) * tm)
        krows = slice(BLOCK + hf * tm, BLOCK + (hf + 1) * tm)
        st = {}

        def norm():
            st["h"] = _modulated_norm(x_ref[0, toks, :], shift, scale, g_norm)
            ang = pos_ref[pl.ds(batch, 1), toks].astype(jnp.float32) * invf
            st["cos"], st["sin"] = jnp.cos(ang), jnp.sin(ang)

        def norm_rope(t, gain):
            ssq = jnp.sum(t * t, axis=0, keepdims=True)
            tn = t * lax.rsqrt(ssq * (1.0 / HEAD_DIM) + NORM_EPS) * gain
            x1, x2 = tn[0:ROT_HALF], tn[ROT_HALF:ROT_DIM]
            cos, sin = st["cos"], st["sin"]
            return jnp.concatenate([x1 * cos - x2 * sin, x2 * cos + x1 * sin, tn[ROT_DIM:]], axis=0)

        def chunk(c):
            row0 = c * PROJ_CHUNK
            pt = lax.dot_general(w_in_t_s[row0:row0 + PROJ_CHUNK, :], st["h"], _NT,
                                 preferred_element_type=jnp.float32)
            if row0 >= Q_WIDTH + 2 * KV_WIDTH:
                g0 = row0 - (Q_WIDTH + 2 * KV_WIDTH)
                gt_s[g0:g0 + PROJ_CHUNK, toks] = _silu(pt)
                return
            for i in range(PROJ_CHUNK // HEAD_DIM):
                t = pt[i * HEAD_DIM:(i + 1) * HEAD_DIM]
                hd = row0 // HEAD_DIM + i
                if hd < N_HEADS:
                    qn = norm_rope(t, q_gain)
                    keep = low_q if hd % 2 == 0 else ~low_q
                    swapped = jnp.concatenate(
                        [pltpu.roll(qn[:, c2 * LANES:(c2 + 1) * LANES], SUB, 1)
                         for c2 in range(tm // LANES)], axis=1)
                    hrows = slice(hd * HEAD_DIM, (hd + 1) * HEAD_DIM)
                    qa_s[hrows, toks] = jnp.where(keep, qn, 0.0).astype(jnp.bfloat16)
                    qb_s[hrows, toks] = jnp.where(keep, swapped, 0.0).astype(jnp.bfloat16)
                elif hd < N_HEADS + N_KV_HEADS:
                    kh = hd - N_HEADS
                    kt = norm_rope(t, k_gain)
                    k_s[krows, kh * LANES:(kh + 1) * LANES] = (
                        jnp.concatenate([kt, kt], axis=0).T.astype(jnp.bfloat16))
                else:
                    kh = hd - N_HEADS - N_KV_HEADS
                    vt_s[kh * V_ROWS:kh * V_ROWS + HEAD_DIM, krows] = t.astype(jnp.bfloat16)
                    vt_s[kh * V_ROWS + HEAD_DIM:(kh + 1) * V_ROWS, krows] = ones

        return [norm] + [(lambda c=c: chunk(c)) for c in range(ATTN_IN_WIDTH // PROJ_CHUNK)]

    def scores(qb):
        qcols = slice(qb * BLOCK, (qb + 1) * BLOCK)
        out = []
        for kh in range(N_KV_HEADS):
            tiles = []
            for hq in range(2):
                u = 2 * qb + hq
                kd = k_s[u * SUB:(u + 3) * SUB, kh * LANES:(kh + 1) * LANES]
                blocks = []
                for g in range(GROUP):
                    hrows = slice((kh * GROUP + g) * HEAD_DIM, (kh * GROUP + g + 1) * HEAD_DIM)
                    src = (qa_s if g % 2 == hq else qb_s)[hrows, qcols]
                    blocks.append(jnp.concatenate([src, zero] if g < 2 else [zero, src], axis=1))
                sc = jnp.dot(jnp.concatenate([kd, kd], axis=1), jnp.concatenate(blocks, axis=0),
                             preferred_element_type=jnp.float32)
                old, mid, new = sc[:SUB], sc[SUB:2 * SUB], sc[2 * SUB:]
                if qb == 0:
                    old = old + prev_bias
                    if hq == 0:
                        mid = mid + prev_bias
                tiles.append(jnp.concatenate([mid, jnp.where(use_new, new, old)], axis=0))
            out.append(tiles)
        return out

    def finish(qb, tiles_all):
        qcols = slice(qb * BLOCK, (qb + 1) * BLOCK)
        keys = slice(qb * BLOCK, (qb + 2) * BLOCK)
        for kh, tiles in enumerate(tiles_all):
            vt = vt_s[kh * V_ROWS:(kh + 1) * V_ROWS, keys]
            h0 = kh * GROUP
            sink = jnp.where(lane4 == 0, sink_ref[attn_layer, h0], jnp.where(
                lane4 == 1, sink_ref[attn_layer, h0 + 1], jnp.where(
                    lane4 == 2, sink_ref[attn_layer, h0 + 2],
                    sink_ref[attn_layer, h0 + 3]))) * LOG2E
            outs = []
            for hq, sc in enumerate(tiles):
                m = jnp.maximum(jnp.max(sc, axis=0, keepdims=True), sink)
                p = jnp.exp2(sc - m).astype(jnp.bfloat16)
                p_mid, p_fold = p[:SUB], p[SUB:]
                window = [jnp.where(use_new, zero_p, p_fold), p_mid, jnp.where(use_new, p_fold, zero_p)]
                p2 = jnp.concatenate(window + [zero_p] if hq == 0 else [zero_p] + window, axis=0)
                o = jnp.dot(vt, p2, preferred_element_type=jnp.float32)
                denom = o[HEAD_DIM:HEAD_DIM + 1] + jnp.exp2(sink - m)
                outs.append(o[:HEAD_DIM] * (1.0 / denom))
            for g in range(GROUP):
                par = g % 2
                cv = slice((g // 2) * LANES, (g // 2 + 1) * LANES)
                moved = pltpu.roll(outs[1 - par][:, cv], SUB, 1)
                full = jnp.where(low_o if par == 0 else ~low_o, outs[par][:, cv], moved)
                hrows = slice((h0 + g) * HEAD_DIM, (h0 + g + 1) * HEAD_DIM)
                ogt_s[hrows, qcols] = (full * gt_s[hrows, qcols]).astype(jnp.bfloat16)

    def attention(hf):
        blocks = range(hf * (tm // BLOCK), (hf + 1) * (tm // BLOCK))
        st = {}

        def score_step(qb):
            st[qb] = scores(qb)

        def finish_step(qb):
            finish(qb, st.pop(qb))

        steps = [lambda: score_step(blocks[0])]
        for qb in blocks:
            if qb + 1 in blocks:
                steps.append(lambda qb=qb: score_step(qb + 1))
            steps.append(lambda qb=qb: finish_step(qb))
        return steps

    def out_projection(hf):
        toks = slice(hf * tm, (hf + 1) * tm)

        def run():
            out = lax.dot_general(ogt_s[:, toks], w_out_s[...], _TN,
                                  preferred_element_type=jnp.float32)
            o_ref[0, toks, :] = x_ref[0, toks, :] + gate * out

        return [run]

    def interleave(a, b):
        done_a = done_b = 0
        while done_a < len(a) or done_b < len(b):
            if done_b >= len(b) or (done_a < len(a) and done_a * len(b) <= done_b * len(a)):
                a[done_a]()
                done_a += 1
            else:
                b[done_b]()
                done_b += 1

    proj0, proj1 = projection(0), projection(1)
    for f in proj0:
        f()
    proj1[0]()
    interleave(attention(0), proj1[1:])
    interleave(attention(1), out_projection(0))
    out_projection(1)[0]()

    step_tokens = 2 * tm
    k_s[0:BLOCK, :] = k_s[step_tokens:step_tokens + BLOCK, :]
    vt_s[:, 0:BLOCK] = vt_s[:, step_tokens:step_tokens + BLOCK]


def _attn_small_params(q_norm, k_norm):
    inv_freq = ROPE_THETA ** (-jnp.arange(ROT_HALF, dtype=jnp.float32) * 2.0 / ROT_DIM)
    cols = [inv_freq]
    for l in range(q_norm.shape[0]):
        cols += [q_norm[l] * (HEAD_DIM ** -0.5 * LOG2E), k_norm[l]]
    col = jnp.concatenate(cols)
    return jnp.broadcast_to(col[:, None], (col.shape[0], LANES))


def _attn_layer(x, mod_all, mod_idx, norm_g, positions, small, w_in_all, sinks, w_out_all,
                layer, attn_layer, later_adaln=None):
    b, s, d = x.shape
    tm = ATTN_HALF_TILE
    step = 2 * tm
    steps_per_seq = s // step
    const = lambda i, j: (0, 0)
    out_specs = pl.BlockSpec((1, step, d), lambda i, j: (i, j, 0))
    out_shape = jax.ShapeDtypeStruct(x.shape, x.dtype)
    later_specs, later_args, n_later = [], [], 0
    if later_adaln is not None:
        c, ada_w, ada_b, first_layer = later_adaln
        assert first_layer == layer + 1
        n_later = ada_w.shape[0] - first_layer
        n_steps = b * steps_per_seq
        rows = d // n_steps
        assert rows * n_steps == d and rows % 8 == 0
        flat = lambda i, j: i * steps_per_seq + j
        later_specs = (
            [pl.BlockSpec((1, b, rows), lambda i, j: (flat(i, j), 0, 0))]
            + [pl.BlockSpec((1, rows, 3 * d), lambda i, j, l=l: (first_layer + l, flat(i, j), 0))
               for l in range(n_later)]
            + [pl.BlockSpec(ada_b.shape, const)])
        later_args = [c.reshape(b, n_steps, rows).transpose(1, 0, 2)] + [ada_w] * n_later + [ada_b]
        out_specs = [out_specs, pl.BlockSpec((n_later, b, 3 * d), lambda i, j: (0, 0, 0))]
        out_shape = [out_shape, jax.ShapeDtypeStruct((n_later, b, 3 * d), jnp.float32)]
    return pl.pallas_call(
        functools.partial(_attn_kernel, n_later=n_later, layer=layer, attn_layer=attn_layer),
        grid=(b, steps_per_seq),
        in_specs=[
            pl.BlockSpec((1, step, d), lambda i, j: (i, j, 0)),
            pl.BlockSpec((1, b, 3 * d), lambda i, j: (mod_idx, 0, 0)),
            pl.BlockSpec(norm_g.shape, const),
            pl.BlockSpec((b, step), lambda i, j: (0, j)),
            pl.BlockSpec(small.shape, const),
            pl.BlockSpec((1, d, ATTN_IN_WIDTH), lambda i, j: (attn_layer, 0, 0)),
            pl.BlockSpec((1, Q_WIDTH, d), lambda i, j: (attn_layer, 0, 0)),
            pl.BlockSpec(memory_space=pltpu.SMEM),
        ] + later_specs,
        out_specs=out_specs,
        out_shape=out_shape,
        scratch_shapes=[
            pltpu.VMEM((Q_WIDTH, step), jnp.bfloat16),
            pltpu.VMEM((Q_WIDTH, step), jnp.bfloat16),
            pltpu.VMEM((BLOCK + step, N_KV_HEADS * LANES), jnp.bfloat16),
            pltpu.VMEM((N_KV_HEADS * V_ROWS, BLOCK + step), jnp.bfloat16),
            pltpu.VMEM((Q_WIDTH, step), jnp.float32),
            pltpu.VMEM((Q_WIDTH, step), jnp.bfloat16),
            pltpu.VMEM((ATTN_IN_WIDTH, d), jnp.bfloat16),
            pltpu.VMEM((Q_WIDTH, d), jnp.bfloat16),
        ],
        compiler_params=pltpu.CompilerParams(
            dimension_semantics=("arbitrary", "arbitrary"), vmem_limit_bytes=VMEM_LIMIT),
        name="attn_layer",
    )(x, mod_all, norm_g, positions, small, w_in_all, w_out_all, sinks, *later_args)


def _pool_kernel(x_ref, mod_ref, g_ref, w_in_ref, w_grp_ref, scale_ref, w_out_ref, o_ref,
                 v_s, w_in_s, w_grp_s, w_out_s, *, layer, pool_layer):
    tm = POOL_HALF_TILE
    j = pl.program_id(1)
    shift, scale, gate = _modulation(mod_ref, pl.program_id(0))
    g_norm = g_ref[layer:layer + 1, :]
    mix_scale = scale_ref[pool_layer:pool_layer + 1, :]

    @pl.when((pl.program_id(0) == 0) & (j == 0))
    def _():
        for c in range(D_MODEL // LANES):
            rows = slice(c * LANES, (c + 1) * LANES)
            w_in_s[rows, :] = w_in_ref[0, rows, :].astype(jnp.bfloat16)
            w_out_s[rows, :] = w_out_ref[0, rows, :].astype(jnp.bfloat16)
        for gi in range(len(POOL_WINDOWS)):
            w_grp_s[gi] = w_grp_ref[0, gi].astype(jnp.bfloat16)

    @pl.when(j == 0)
    def _():
        v_s[0:POOL_HALO, :] = jnp.zeros((POOL_HALO, v_s.shape[1]), v_s.dtype)

    gates = {}

    def projection(hf):
        toks = slice(hf * tm, (hf + 1) * tm)
        st = {}

        def norm():
            st["h"] = _modulated_norm(x_ref[0, toks, :], shift, scale, g_norm)

        def value_path():
            v_s[POOL_HALO + hf * tm:POOL_HALO + (hf + 1) * tm, :] = jnp.dot(
                st["h"], w_in_s[:, :D_MODEL], preferred_element_type=jnp.float32)

        def gate_path():
            gates[hf] = _silu(jnp.dot(st["h"], w_in_s[:, D_MODEL:],
                                      preferred_element_type=jnp.float32))

        return [norm, value_path, gate_path]

    def mixing(hf):
        toks = slice(hf * tm, (hf + 1) * tm)
        ext = slice(hf * tm, POOL_HALO + (hf + 1) * tm)
        t = lax.broadcasted_iota(jnp.int32, (tm, POOL_GROUP_DIM), 0) + (j * 2 + hf) * tm
        mixed = []

        def group(gi):
            w = POOL_WINDOWS[gi]
            cols = slice(gi * POOL_GROUP_DIM, (gi + 1) * POOL_GROUP_DIM)
            win = v_s[ext, cols]
            cur = win[POOL_HALO:]
            span = 1
            while span < w:
                win = win + pltpu.roll(win, span, 0)
                span *= 2
            count = jnp.minimum(t + 1, w).astype(jnp.float32)
            pooled = win[POOL_HALO:] / count - cur
            mixed.append(jnp.dot(pooled.astype(jnp.bfloat16), w_grp_s[gi],
                                 preferred_element_type=jnp.float32))

        def out():
            m = jnp.concatenate(mixed, axis=1) * mix_scale
            o = jnp.dot((m * gates[hf]).astype(jnp.bfloat16), w_out_s[...],
                        preferred_element_type=jnp.float32)
            o_ref[0, toks, :] = x_ref[0, toks, :] + gate * o

        return [(lambda gi=gi: group(gi)) for gi in range(len(POOL_WINDOWS))] + [out]

    def interleave(a, b):
        done_a = done_b = 0
        while done_a < len(a) or done_b < len(b):
            if done_b >= len(b) or (done_a < len(a) and done_a * len(b) <= done_b * len(a)):
                a[done_a]()
                done_a += 1
            else:
                b[done_b]()
                done_b += 1

    proj0, proj1 = projection(0), projection(1)
    for f in proj0:
        f()
    proj1[0]()
    interleave(mixing(0), proj1[1:])
    for f in mixing(1):
        f()

    v_s[0:POOL_HALO, :] = v_s[2 * tm:2 * tm + POOL_HALO, :]


def _pool_layer(x, mod_all, mod_idx, norm_g, w_in_all, w_group_all, scale_all, w_out_all,
                layer, pool_layer):
    b, s, d = x.shape
    tm = 2 * POOL_HALF_TILE
    n_grp = len(POOL_WINDOWS)
    const = lambda i, j: (0, 0)
    return pl.pallas_call(
        functools.partial(_pool_kernel, layer=layer, pool_layer=pool_layer),
        grid=(b, s // tm),
        in_specs=[
            pl.BlockSpec((1, tm, d), lambda i, j: (i, j, 0)),
            pl.BlockSpec((1, b, 3 * d), lambda i, j: (mod_idx, 0, 0)),
            pl.BlockSpec(norm_g.shape, const),
            pl.BlockSpec((1, d, 2 * D_MODEL), lambda i, j: (pool_layer, 0, 0)),
            pl.BlockSpec((1, n_grp, POOL_GROUP_DIM, POOL_GROUP_DIM), lambda i, j: (pool_layer, 0, 0, 0)),
            pl.BlockSpec(scale_all.shape, const),
            pl.BlockSpec((1, D_MODEL, d), lambda i, j: (pool_layer, 0, 0)),
        ],
        out_specs=pl.BlockSpec((1, tm, d), lambda i, j: (i, j, 0)),
        out_shape=jax.ShapeDtypeStruct(x.shape, x.dtype),
        scratch_shapes=[
            pltpu.VMEM((POOL_HALO + tm, D_MODEL), jnp.float32),
            pltpu.VMEM((d, 2 * D_MODEL), jnp.bfloat16),
            pltpu.VMEM((n_grp, POOL_GROUP_DIM, POOL_GROUP_DIM), jnp.bfloat16),
            pltpu.VMEM((D_MODEL, d), jnp.bfloat16),
        ],
        compiler_params=pltpu.CompilerParams(
            dimension_semantics=("arbitrary", "arbitrary"), vmem_limit_bytes=VMEM_LIMIT),
        name="pool_layer",
    )(x, mod_all, norm_g, w_in_all, w_group_all, scale_all, w_out_all)


def kernel(x, c, positions, ada_w, ada_b, norm_g, attn_w_in, attn_q_norm, attn_k_norm, attn_sinks,
           attn_w_out, pool_w_in, pool_w_group, pool_scale, pool_w_out):
    depth = ada_w.shape[0]
    small = _attn_small_params(attn_q_norm, attn_k_norm)
    mod0 = _adaln(c, ada_w, ada_b, 1)
    mods = [(mod0, 0)]
    for i in range(depth):
        jl = i // 2
        mod_all, mod_idx = mods[i]
        if i % 2 == 0:
            later = (c, ada_w, ada_b, 1) if (i == 0 and depth > 1) else None
            x = _attn_layer(x, mod_all, mod_idx, norm_g, positions, small, attn_w_in, attn_sinks,
                            attn_w_out, i, jl, later)
            if later is not None:
                x, later_mods = x
                mods += [(later_mods, l) for l in range(depth - 1)]
        else:
            x = _pool_layer(x, mod_all, mod_idx, norm_g, pool_w_in, pool_w_group, pool_scale,
                            pool_w_out, i, jl)
    return x
```

```python
import functools
import math

import jax
import jax.numpy as jnp
from jax import lax
from jax.experimental import pallas as pl
from jax.experimental.pallas import tpu as pltpu

D_MODEL = 1024
HEAD_DIM = 64
N_HEADS = 16
N_KV_HEADS = 4
GROUP = N_HEADS // N_KV_HEADS
Q_WIDTH = N_HEADS * HEAD_DIM
KV_WIDTH = N_KV_HEADS * HEAD_DIM
ATTN_IN_WIDTH = 2 * Q_WIDTH + 2 * KV_WIDTH
BLOCK = 128
SUB = BLOCK // 2
ROT_DIM = HEAD_DIM // 4
ROT_HALF = ROT_DIM // 2
ROPE_THETA = 500000.0
POOL_WINDOWS = (2, 4, 8, 16)
POOL_GROUP_DIM = D_MODEL // len(POOL_WINDOWS)
POOL_HALO = 16
NORM_EPS = 1e-6
LANES = 128
BF16_ROWS = 16
V_PAD = BF16_ROWS
V_ROWS = HEAD_DIM + V_PAD
LOG2E = math.log2(math.e)

ATTN_HALF_TILE = 256
ATTN_PARTS = 4
PROJ_CHUNK = 256
POOL_PART_TILE = 512
POOL_PARTS = 2
VMEM_LIMIT = 60 * 1024 * 1024

_NT = (((1,), (1,)), ((), ()))
_TN = (((0,), (0,)), ((), ()))


def _silu(x):
    hx = 0.5 * x
    return hx + hx * jnp.tanh(hx)


def _adaln_kernel(c_ref, w_ref, b_ref, o_ref):
    a = _silu(c_ref[...]).astype(jnp.bfloat16)
    w = w_ref[0].astype(jnp.bfloat16)
    bias = b_ref[pl.ds(pl.program_id(0), 1), :]
    o_ref[0] = jnp.dot(a, w, preferred_element_type=jnp.float32) + bias


def _adaln(c, ada_w, ada_b, depth):
    _, d, n = ada_w.shape
    b = c.shape[0]
    tn = 1024
    return pl.pallas_call(
        _adaln_kernel,
        grid=(depth, n // tn),
        in_specs=[
            pl.BlockSpec((b, d), lambda l, j: (0, 0)),
            pl.BlockSpec((1, d, tn), lambda l, j: (l, 0, j)),
            pl.BlockSpec((ada_b.shape[0], tn), lambda l, j: (0, j)),
        ],
        out_specs=pl.BlockSpec((1, b, tn), lambda l, j: (l, 0, j)),
        out_shape=jax.ShapeDtypeStruct((depth, b, n), jnp.float32),
        compiler_params=pltpu.CompilerParams(
            dimension_semantics=("arbitrary", "arbitrary"), vmem_limit_bytes=VMEM_LIMIT),
        name="adaln_mod",
    )(c, ada_w, ada_b)


def _modulation(mod_ref, batch):
    row = mod_ref[0, pl.ds(batch, 1), :]
    return row[:, :D_MODEL], row[:, D_MODEL:2 * D_MODEL], row[:, 2 * D_MODEL:]


def _modulated_norm(x, shift, scale, g):
    ms = jnp.mean(x * x, axis=-1, keepdims=True)
    gain = g * (1.0 + scale)
    h = x * lax.rsqrt(ms + NORM_EPS) * gain + shift
    return h.astype(jnp.bfloat16)


def _attn_kernel(*refs, n_later, layer, attn_layer):
    n_in = 8
    x_ref, mod_ref, g_ref, pos_ref, small_ref, w_in_ref, w_out_ref, sink_ref = refs[:n_in]
    later = refs[n_in:n_in + (n_later + 2 if n_later else 0)]
    refs = refs[n_in + len(later):]
    o_ref = refs[0]
    later_mod_ref = refs[1] if n_later else None
    qa_s, qb_s, k_s, vt_s, gt_s, ogt_s, w_in_t_s, w_out_s = refs[-8:]
    tm = ATTN_HALF_TILE
    batch = pl.program_id(0)
    j = pl.program_id(1)

    shift, scale, gate = _modulation(mod_ref, batch)
    g_norm = g_ref[layer:layer + 1, :]
    lane_tile = lambda a: jnp.concatenate([a] * (tm // LANES), axis=1)
    invf = lane_tile(small_ref[0:ROT_HALF, :])
    g0 = ROT_HALF + attn_layer * 2 * HEAD_DIM
    q_gain = lane_tile(small_ref[g0:g0 + HEAD_DIM, :])
    k_gain = lane_tile(small_ref[g0 + HEAD_DIM:g0 + 2 * HEAD_DIM, :])

    if n_later:
        c_ref, b_ref = later[0], later[-1]

        @pl.when((pl.program_id(0) == 0) & (j == 0))
        def _():
            for l in range(n_later):
                later_mod_ref[l] = jnp.broadcast_to(b_ref[layer + 1 + l:layer + 2 + l, :],
                                                    later_mod_ref.shape[1:])

        a = _silu(c_ref[0]).astype(jnp.bfloat16)
        for l in range(n_later):
            later_mod_ref[l] += jnp.dot(a, later[1 + l][0].astype(jnp.bfloat16),
                                        preferred_element_type=jnp.float32)

    @pl.when((pl.program_id(0) == 0) & (j == 0))
    def _():
        for c in range(ATTN_IN_WIDTH // LANES):
            cols = slice(c * LANES, (c + 1) * LANES)
            w_in_t_s[cols, :] = w_in_ref[0, :, cols].T.astype(jnp.bfloat16)
        for c in range(Q_WIDTH // LANES):
            rows = slice(c * LANES, (c + 1) * LANES)
            w_out_s[rows, :] = w_out_ref[0, rows, :].astype(jnp.bfloat16)

    @pl.when(j == 0)
    def _():
        k_s[0:BLOCK, :] = jnp.zeros((BLOCK, k_s.shape[1]), k_s.dtype)
        vt_s[:, 0:BLOCK] = jnp.zeros((vt_s.shape[0], BLOCK), vt_s.dtype)

    kq = lax.broadcasted_iota(jnp.int32, (SUB, 2 * BLOCK), 0)
    iq = lax.broadcasted_iota(jnp.int32, (SUB, 2 * BLOCK), 1) & (SUB - 1)
    use_new = kq <= iq
    prev_bias = jnp.where(j > 0, 0.0, -jnp.inf)
    lane4 = lax.broadcasted_iota(jnp.int32, (1, 2 * BLOCK), 1) // SUB
    low_q = (lax.broadcasted_iota(jnp.int32, (HEAD_DIM, tm), 1) & SUB) == 0
    low_o = lax.broadcasted_iota(jnp.int32, (HEAD_DIM, LANES), 1) < SUB
    zero = jnp.zeros((HEAD_DIM, LANES), jnp.bfloat16)
    zero_p = jnp.zeros((SUB, 2 * BLOCK), jnp.bfloat16)
    ones = jnp.ones((V_PAD, tm), jnp.bfloat16)

    def projection(hf):
        toks = slice(hf * tm, (hf + 1) * tm)
        krows = slice(BLOCK + hf * tm, BLOCK + (hf + 1) * tm)
        st = {}

        def norm():
            st["h"] = _modulated_norm(x_ref[0, toks, :], shift, scale, g_norm)
            ang = pos_ref[pl.ds(batch, 1), toks].astype(jnp.float32) * invf
            st["cos"], st["sin"] = jnp.cos(ang), jnp.sin(ang)

        def norm_rope(t, gain):
            ssq = jnp.sum(t * t, axis=0, keepdims=True)
            tn = t * lax.rsqrt(ssq * (1.0 / HEAD_DIM) + NORM_EPS) * gain
            x1, x2 = tn[0:ROT_HALF], tn[ROT_HALF:ROT_DIM]
            cos, sin = st["cos"], st["sin"]
            return jnp.concatenate([x1 * cos - x2 * sin, x2 * cos + x1 * sin, tn[ROT_DIM:]], axis=0)

        def chunk(c):
            row0 = c * PROJ_CHUNK
            pt = lax.dot_general(w_in_t_s[row0:row0 + PROJ_CHUNK, :], st["h"], _NT,
                                 preferred_element_type=jnp.float32)
            if row0 >= Q_WIDTH + 2 * KV_WIDTH:
                g0 = row0 - (Q_WIDTH + 2 * KV_WIDTH)
                gt_s[g0:g0 + PROJ_CHUNK, toks] = _silu(pt)
                return
            for i in range(PROJ_CHUNK // HEAD_DIM):
                t = pt[i * HEAD_DIM:(i + 1) * HEAD_DIM]
                hd = row0 // HEAD_DIM + i
                if hd < N_HEADS:
                    qn = norm_rope(t, q_gain)
                    keep = low_q if hd % 2 == 0 else ~low_q
                    swapped = jnp.concatenate(
                        [pltpu.roll(qn[:, c2 * LANES:(c2 + 1) * LANES], SUB, 1)
                         for c2 in range(tm // LANES)], axis=1)
                    hrows = slice(hd * HEAD_DIM, (hd + 1) * HEAD_DIM)
                    qa_s[hrows, toks] = jnp.where(keep, qn, 0.0).astype(jnp.bfloat16)
                    qb_s[hrows, toks] = jnp.where(keep, swapped, 0.0).astype(jnp.bfloat16)
                elif hd < N_HEADS + N_KV_HEADS:
                    kh = hd - N_HEADS
                    kt = norm_rope(t, k_gain)
                    k_s[krows, kh * LANES:(kh + 1) * LANES] = (
                        jnp.concatenate([kt, kt], axis=0).T.astype(jnp.bfloat16))
                else:
                    kh = hd - N_HEADS - N_KV_HEADS
                    vt_s[kh * V_ROWS:kh * V_ROWS + HEAD_DIM, krows] = t.astype(jnp.bfloat16)
                    vt_s[kh * V_ROWS + HEAD_DIM:(kh + 1) * V_ROWS, krows] = ones

        return [norm] + [(lambda c=c: chunk(c)) for c in range(ATTN_IN_WIDTH // PROJ_CHUNK)]

    def scores(qb):
        qcols = slice(qb * BLOCK, (qb + 1) * BLOCK)
        out = []
        for kh in range(N_KV_HEADS):
            tiles = []
            for hq in range(2):
                u = 2 * qb + hq
                kd = k_s[u * SUB:(u + 3) * SUB, kh * LANES:(kh + 1) * LANES]
                blocks = []
                for g in range(GROUP):
                    hrows = slice((kh * GROUP + g) * HEAD_DIM, (kh * GROUP + g + 1) * HEAD_DIM)
                    src = (qa_s if g % 2 == hq else qb_s)[hrows, qcols]
                    blocks.append(jnp.concatenate([src, zero] if g < 2 else [zero, src], axis=1))
                sc = jnp.dot(jnp.concatenate([kd, kd], axis=1), jnp.concatenate(blocks, axis=0),
                             preferred_element_type=jnp.float32)
                old, mid, new = sc[:SUB], sc[SUB:2 * SUB], sc[2 * SUB:]
                if qb == 0:
                    old = old + prev_bias
                    if hq == 0:
                        mid = mid + prev_bias
                tiles.append(jnp.concatenate([mid, jnp.where(use_new, new, old)], axis=0))
            out.append(tiles)
        return out

    def finish(qb, tiles_all):
        qcols = slice(qb * BLOCK, (qb + 1) * BLOCK)
        keys = slice(qb * BLOCK, (qb + 2) * BLOCK)
        for kh, tiles in enumerate(tiles_all):
            vt = vt_s[kh * V_ROWS:(kh + 1) * V_ROWS, keys]
            h0 = kh * GROUP
            sink = jnp.where(lane4 == 0, sink_ref[attn_layer, h0], jnp.where(
                lane4 == 1, sink_ref[attn_layer, h0 + 1], jnp.where(
                    lane4 == 2, sink_ref[attn_layer, h0 + 2],
                    sink_ref[attn_layer, h0 + 3]))) * LOG2E
            outs = []
            for hq, sc in enumerate(tiles):
                m = jnp.maximum(jnp.max(sc, axis=0, keepdims=True), sink)
                p = jnp.exp2(sc - m).astype(jnp.bfloat16)
                p_mid, p_fold = p[:SUB], p[SUB:]
                window = [jnp.where(use_new, zero_p, p_fold), p_mid, jnp.where(use_new, p_fold, zero_p)]
                p2 = jnp.concatenate(window + [zero_p] if hq == 0 else [zero_p] + window, axis=0)
                o = jnp.dot(vt, p2, preferred_element_type=jnp.float32)
                denom = o[HEAD_DIM:HEAD_DIM + 1] + jnp.exp2(sink - m)
                outs.append(o[:HEAD_DIM] * (1.0 / denom))
            for g in range(GROUP):
                par = g % 2
                cv = slice((g // 2) * LANES, (g // 2 + 1) * LANES)
                moved = pltpu.roll(outs[1 - par][:, cv], SUB, 1)
                full = jnp.where(low_o if par == 0 else ~low_o, outs[par][:, cv], moved)
                hrows = slice((h0 + g) * HEAD_DIM, (h0 + g + 1) * HEAD_DIM)
                ogt_s[hrows, qcols] = (full * gt_s[hrows, qcols]).astype(jnp.bfloat16)

    def attention(hf):
        blocks = range(hf * (tm // BLOCK), (hf + 1) * (tm // BLOCK))
        st = {}

        def score_step(qb):
            st[qb] = scores(qb)

        def finish_step(qb):
            finish(qb, st.pop(qb))

        steps = [lambda: score_step(blocks[0])]
        for qb in blocks:
            if qb + 1 in blocks:
                steps.append(lambda qb=qb: score_step(qb + 1))
            steps.append(lambda qb=qb: finish_step(qb))
        return steps

    def out_projection(hf):
        toks = slice(hf * tm, (hf + 1) * tm)

        def run():
            out = lax.dot_general(ogt_s[:, toks], w_out_s[...], _TN,
                                  preferred_element_type=jnp.float32)
            o_ref[0, toks, :] = x_ref[0, toks, :] + gate * out

        return [run]

    def interleave(a, b):
        done_a = done_b = 0
        while done_a < len(a) or done_b < len(b):
            if done_b >= len(b) or (done_a < len(a) and done_a * len(b) <= done_b * len(a)):
                a[done_a]()
                done_a += 1
            else:
                b[done_b]()
                done_b += 1

    for f in projection(0):
        f()
    for part in range(1, ATTN_PARTS):
        proj = projection(part)
        proj[0]()
        done = out_projection(part - 2) if part >= 2 else []
        interleave(attention(part - 1) + done, proj[1:])
    done = out_projection(ATTN_PARTS - 2) if ATTN_PARTS >= 2 else []
    interleave(attention(ATTN_PARTS - 1), done)
    out_projection(ATTN_PARTS - 1)[0]()

    step_tokens = ATTN_PARTS * tm
    k_s[0:BLOCK, :] = k_s[step_tokens:step_tokens + BLOCK, :]
    vt_s[:, 0:BLOCK] = vt_s[:, step_tokens:step_tokens + BLOCK]


def _attn_small_params(q_norm, k_norm):
    inv_freq = ROPE_THETA ** (-jnp.arange(ROT_HALF, dtype=jnp.float32) * 2.0 / ROT_DIM)
    cols = [inv_freq]
    for l in range(q_norm.shape[0]):
        cols += [q_norm[l] * (HEAD_DIM ** -0.5 * LOG2E), k_norm[l]]
    col = jnp.concatenate(cols)
    return jnp.broadcast_to(col[:, None], (col.shape[0], LANES))


def _attn_layer(x, mod_all, mod_idx, norm_g, positions, small, w_in_all, sinks, w_out_all,
                layer, attn_layer, later_adaln=None):
    b, s, d = x.shape
    tm = ATTN_HALF_TILE
    step = ATTN_PARTS * tm
    steps_per_seq = s // step
    const = lambda i, j: (0, 0)
    out_specs = pl.BlockSpec((1, step, d), lambda i, j: (i, j, 0))
    out_shape = jax.ShapeDtypeStruct(x.shape, x.dtype)
    later_specs, later_args, n_later = [], [], 0
    if later_adaln is not None:
        c, ada_w, ada_b, first_layer = later_adaln
        assert first_layer == layer + 1
        n_later = ada_w.shape[0] - first_layer
        n_steps = b * steps_per_seq
        rows = d // n_steps
        assert rows * n_steps == d and rows % 8 == 0
        flat = lambda i, j: i * steps_per_seq + j
        later_specs = (
            [pl.BlockSpec((1, b, rows), lambda i, j: (flat(i, j), 0, 0))]
            + [pl.BlockSpec((1, rows, 3 * d), lambda i, j, l=l: (first_layer + l, flat(i, j), 0))
               for l in range(n_later)]
            + [pl.BlockSpec(ada_b.shape, const)])
        later_args = [c.reshape(b, n_steps, rows).transpose(1, 0, 2)] + [ada_w] * n_later + [ada_b]
        out_specs = [out_specs, pl.BlockSpec((n_later, b, 3 * d), lambda i, j: (0, 0, 0))]
        out_shape = [out_shape, jax.ShapeDtypeStruct((n_later, b, 3 * d), jnp.float32)]
    return pl.pallas_call(
        functools.partial(_attn_kernel, n_later=n_later, layer=layer, attn_layer=attn_layer),
        grid=(b, steps_per_seq),
        in_specs=[
            pl.BlockSpec((1, step, d), lambda i, j: (i, j, 0)),
            pl.BlockSpec((1, b, 3 * d), lambda i, j: (mod_idx, 0, 0)),
            pl.BlockSpec(norm_g.shape, const),
            pl.BlockSpec((b, step), lambda i, j: (0, j)),
            pl.BlockSpec(small.shape, const),
            pl.BlockSpec((1, d, ATTN_IN_WIDTH), lambda i, j: (attn_layer, 0, 0)),
            pl.BlockSpec((1, Q_WIDTH, d), lambda i, j: (attn_layer, 0, 0)),
            pl.BlockSpec(memory_space=pltpu.SMEM),
        ] + later_specs,
        out_specs=out_specs,
        out_shape=out_shape,
        scratch_shapes=[
            pltpu.VMEM((Q_WIDTH, step), jnp.bfloat16),
            pltpu.VMEM((Q_WIDTH, step), jnp.bfloat16),
            pltpu.VMEM((BLOCK + step, N_KV_HEADS * LANES), jnp.bfloat16),
            pltpu.VMEM((N_KV_HEADS * V_ROWS, BLOCK + step), jnp.bfloat16),
            pltpu.VMEM((Q_WIDTH, step), jnp.float32),
            pltpu.VMEM((Q_WIDTH, step), jnp.bfloat16),
            pltpu.VMEM((ATTN_IN_WIDTH, d), jnp.bfloat16),
            pltpu.VMEM((Q_WIDTH, d), jnp.bfloat16),
        ],
        compiler_params=pltpu.CompilerParams(
            dimension_semantics=("arbitrary", "arbitrary"), vmem_limit_bytes=VMEM_LIMIT),
        name="attn_layer",
    )(x, mod_all, norm_g, positions, small, w_in_all, w_out_all, sinks, *later_args)


def _pool_kernel(x_ref, mod_ref, g_ref, w_in_ref, w_grp_ref, scale_ref, w_out_ref, o_ref,
                 v_s, w_in_s, w_grp_s, w_out_s, *, layer, pool_layer):
    tm = POOL_PART_TILE
    j = pl.program_id(1)
    shift, scale, gate = _modulation(mod_ref, pl.program_id(0))
    g_norm = g_ref[layer:layer + 1, :]
    mix_scale = scale_ref[pool_layer:pool_layer + 1, :]

    @pl.when((pl.program_id(0) == 0) & (j == 0))
    def _():
        for c in range(D_MODEL // LANES):
            rows = slice(c * LANES, (c + 1) * LANES)
            w_in_s[rows, :] = w_in_ref[0, rows, :].astype(jnp.bfloat16)
            w_out_s[rows, :] = w_out_ref[0, rows, :].astype(jnp.bfloat16)
        for gi in range(len(POOL_WINDOWS)):
            w_grp_s[gi] = w_grp_ref[0, gi].astype(jnp.bfloat16)

    @pl.when(j == 0)
    def _():
        v_s[0:POOL_HALO, :] = jnp.zeros((POOL_HALO, v_s.shape[1]), v_s.dtype)

    gates = {}

    def projection(hf):
        toks = slice(hf * tm, (hf + 1) * tm)
        st = {}

        def norm():
            st["h"] = _modulated_norm(x_ref[0, toks, :], shift, scale, g_norm)

        def value_path():
            v_s[POOL_HALO + hf * tm:POOL_HALO + (hf + 1) * tm, :] = jnp.dot(
                st["h"], w_in_s[:, :D_MODEL], preferred_element_type=jnp.float32)

        def gate_path():
            gates[hf] = _silu(jnp.dot(st["h"], w_in_s[:, D_MODEL:],
                                      preferred_element_type=jnp.float32))

        return [norm, value_path, gate_path]

    def mixing(hf):
        toks = slice(hf * tm, (hf + 1) * tm)
        ext = slice(hf * tm, POOL_HALO + (hf + 1) * tm)
        t = lax.broadcasted_iota(jnp.int32, (tm, POOL_GROUP_DIM), 0) + (j * POOL_PARTS + hf) * tm
        mixed = []

        def group(gi):
            w = POOL_WINDOWS[gi]
            cols = slice(gi * POOL_GROUP_DIM, (gi + 1) * POOL_GROUP_DIM)
            win = v_s[ext, cols]
            cur = win[POOL_HALO:]
            span = 1
            while span < w:
                win = win + pltpu.roll(win, span, 0)
                span *= 2
            count = jnp.minimum(t + 1, w).astype(jnp.float32)
            pooled = win[POOL_HALO:] / count - cur
            mixed.append(jnp.dot(pooled.astype(jnp.bfloat16), w_grp_s[gi],
                                 preferred_element_type=jnp.float32))

        def out():
            m = jnp.concatenate(mixed, axis=1) * mix_scale
            o = jnp.dot((m * gates[hf]).astype(jnp.bfloat16), w_out_s[...],
                        preferred_element_type=jnp.float32)
            o_ref[0, toks, :] = x_ref[0, toks, :] + gate * o

        return [(lambda gi=gi: group(gi)) for gi in range(len(POOL_WINDOWS))] + [out]

    def interleave(a, b):
        done_a = done_b = 0
        while done_a < len(a) or done_b < len(b):
            if done_b >= len(b) or (done_a < len(a) and done_a * len(b) <= done_b * len(a)):
                a[done_a]()
                done_a += 1
            else:
                b[done_b]()
                done_b += 1

    for f in projection(0):
        f()
    for part in range(1, POOL_PARTS):
        proj = projection(part)
        proj[0]()
        interleave(mixing(part - 1), proj[1:])
    for f in mixing(POOL_PARTS - 1):
        f()

    v_s[0:POOL_HALO, :] = v_s[POOL_PARTS * tm:POOL_PARTS * tm + POOL_HALO, :]


def _pool_layer(x, mod_all, mod_idx, norm_g, w_in_all, w_group_all, scale_all, w_out_all,
                layer, pool_layer):
    b, s, d = x.shape
    tm = POOL_PARTS * POOL_PART_TILE
    n_grp = len(POOL_WINDOWS)
    const = lambda i, j: (0, 0)
    return pl.pallas_call(
        functools.partial(_pool_kernel, layer=layer, pool_layer=pool_layer),
        grid=(b, s // tm),
        in_specs=[
            pl.BlockSpec((1, tm, d), lambda i, j: (i, j, 0)),
            pl.BlockSpec((1, b, 3 * d), lambda i, j: (mod_idx, 0, 0)),
            pl.BlockSpec(norm_g.shape, const),
            pl.BlockSpec((1, d, 2 * D_MODEL), lambda i, j: (pool_layer, 0, 0)),
            pl.BlockSpec((1, n_grp, POOL_GROUP_DIM, POOL_GROUP_DIM), lambda i, j: (pool_layer, 0, 0, 0)),
            pl.BlockSpec(scale_all.shape, const),
            pl.BlockSpec((1, D_MODEL, d), lambda i, j: (pool_layer, 0, 0)),
        ],
        out_specs=pl.BlockSpec((1, tm, d), lambda i, j: (i, j, 0)),
        out_shape=jax.ShapeDtypeStruct(x.shape, x.dtype),
        scratch_shapes=[
            pltpu.VMEM((POOL_HALO + tm, D_MODEL), jnp.float32),
            pltpu.VMEM((d, 2 * D_MODEL), jnp.bfloat16),
            pltpu.VMEM((n_grp, POOL_GROUP_DIM, POOL_GROUP_DIM), jnp.bfloat16),
            pltpu.VMEM((D_MODEL, d), jnp.bfloat16),
        ],
        compiler_params=pltpu.CompilerParams(
            dimension_semantics=("arbitrary", "arbitrary"), vmem_limit_bytes=VMEM_LIMIT),
        name="pool_layer",
    )(x, mod_all, norm_g, w_in_all, w_group_all, scale_all, w_out_all)


def kernel(x, c, positions, ada_w, ada_b, norm_g, attn_w_in, attn_q_norm, attn_k_norm, attn_sinks,
           attn_w_out, pool_w_in, pool_w_group, pool_scale, pool_w_out):
    depth = ada_w.shape[0]
    small = _attn_small_params(attn_q_norm, attn_k_norm)
    mod0 = _adaln(c, ada_w, ada_b, 1)
    mods = [(mod0, 0)]
    for i in range(depth):
        jl = i // 2
        mod_all, mod_idx = mods[i]
        if i % 2 == 0:
            later = (c, ada_w, ada_b, 1) if (i == 0 and depth > 1) else None
            x = _attn_layer(x, mod_all, mod_idx, norm_g, positions, small, attn_w_in, attn_sinks,
                            attn_w_out, i, jl, later)
            if later is not None:
                x, later_mods = x
                mods += [(later_mods, l) for l in range(depth - 1)]
        else:
            x = _pool_layer(x, mod_all, mod_idx, norm_g, pool_w_in, pool_w_group, pool_scale,
                            pool_w_out, i, jl)
    return x
```

```python
import functools
import math

import jax
import jax.numpy as jnp
from jax import lax
from jax.experimental import pallas as pl
from jax.experimental.pallas import tpu as pltpu

D_MODEL = 1024
HEAD_DIM = 64
N_HEADS = 16
N_KV_HEADS = 4
GROUP = N_HEADS // N_KV_HEADS
Q_WIDTH = N_HEADS * HEAD_DIM
KV_WIDTH = N_KV_HEADS * HEAD_DIM
ATTN_IN_WIDTH = 2 * Q_WIDTH + 2 * KV_WIDTH
BLOCK = 128
SUB = BLOCK // 2
ROT_DIM = HEAD_DIM // 4
ROT_HALF = ROT_DIM // 2
ROPE_THETA = 500000.0
POOL_WINDOWS = (2, 4, 8, 16)
POOL_GROUP_DIM = D_MODEL // len(POOL_WINDOWS)
POOL_HALO = 16
NORM_EPS = 1e-6
LANES = 128
BF16_ROWS = 16
V_PAD = BF16_ROWS
V_ROWS = HEAD_DIM + V_PAD
LOG2E = math.log2(math.e)

ATTN_HALF_TILE = 256
ATTN_PARTS = 4
PROJ_CHUNK = 512
POOL_PART_TILE = 512
POOL_PARTS = 2
ADALN_UPFRONT = 2
VMEM_LIMIT = 60 * 1024 * 1024

_NT = (((1,), (1,)), ((), ()))
_TN = (((0,), (0,)), ((), ()))


def _silu(x):
    hx = 0.5 * x
    return hx + hx * jnp.tanh(hx)


def _adaln_kernel(c_ref, w_ref, b_ref, o_ref):
    a = _silu(c_ref[...]).astype(jnp.bfloat16)
    w = w_ref[0].astype(jnp.bfloat16)
    bias = b_ref[pl.ds(pl.program_id(0), 1), :]
    o_ref[0] = jnp.dot(a, w, preferred_element_type=jnp.float32) + bias


def _adaln(c, ada_w, ada_b, depth):
    _, d, n = ada_w.shape
    b = c.shape[0]
    tn = 1024
    return pl.pallas_call(
        _adaln_kernel,
        grid=(depth, n // tn),
        in_specs=[
            pl.BlockSpec((b, d), lambda l, j: (0, 0)),
            pl.BlockSpec((1, d, tn), lambda l, j: (l, 0, j)),
            pl.BlockSpec((ada_b.shape[0], tn), lambda l, j: (0, j)),
        ],
        out_specs=pl.BlockSpec((1, b, tn), lambda l, j: (l, 0, j)),
        out_shape=jax.ShapeDtypeStruct((depth, b, n), jnp.float32),
        compiler_params=pltpu.CompilerParams(
            dimension_semantics=("arbitrary", "arbitrary"), vmem_limit_bytes=VMEM_LIMIT),
        name="adaln_mod",
    )(c, ada_w, ada_b)


def _modulation(mod_ref, batch):
    row = mod_ref[0, pl.ds(batch, 1), :]
    return row[:, :D_MODEL], row[:, D_MODEL:2 * D_MODEL], row[:, 2 * D_MODEL:]


def _modulated_norm(x, shift, scale, g):
    ms = jnp.mean(x * x, axis=-1, keepdims=True)
    gain = g * (1.0 + scale)
    h = x * lax.rsqrt(ms + NORM_EPS) * gain + shift
    return h.astype(jnp.bfloat16)


def _later_adaln_step(later, later_mod_ref, layer):
    c_ref, w_refs, b_ref = later[0], later[1:-1], later[-1]

    @pl.when((pl.program_id(0) == 0) & (pl.program_id(1) == 0))
    def _():
        for l in range(len(w_refs)):
            later_mod_ref[l] = jnp.broadcast_to(b_ref[layer + 1 + l:layer + 2 + l, :],
                                                later_mod_ref.shape[1:])

    a = _silu(c_ref[0]).astype(jnp.bfloat16)
    for l, w_ref in enumerate(w_refs):
        later_mod_ref[l] += jnp.dot(a, w_ref[0].astype(jnp.bfloat16),
                                    preferred_element_type=jnp.float32)


def _later_adaln_operands(later_adaln, layer, b, d, steps_per_seq):
    if later_adaln is None:
        return [], [], [], [], 0
    c, ada_w, ada_b = later_adaln
    first = layer + 1
    n_later = ada_w.shape[0] - first
    n_steps = b * steps_per_seq
    rows = d // n_steps
    assert rows * n_steps == d and rows % 8 == 0
    flat = lambda i, j: i * steps_per_seq + j
    specs = ([pl.BlockSpec((1, b, rows), lambda i, j: (flat(i, j), 0, 0))]
             + [pl.BlockSpec((1, rows, 3 * d), lambda i, j, l=l: (first + l, flat(i, j), 0))
                for l in range(n_later)]
             + [pl.BlockSpec(ada_b.shape, lambda i, j: (0, 0))])
    args = [c.reshape(b, n_steps, rows).transpose(1, 0, 2)] + [ada_w] * n_later + [ada_b]
    out_specs = [pl.BlockSpec((n_later, b, 3 * d), lambda i, j: (0, 0, 0))]
    out_shape = [jax.ShapeDtypeStruct((n_later, b, 3 * d), jnp.float32)]
    return specs, args, out_specs, out_shape, n_later


def _attn_kernel(*refs, n_later, layer, attn_layer):
    n_in = 8
    x_ref, mod_ref, g_ref, pos_ref, small_ref, w_in_ref, w_out_ref, sink_ref = refs[:n_in]
    later = refs[n_in:n_in + (n_later + 2 if n_later else 0)]
    refs = refs[n_in + len(later):]
    o_ref = refs[0]
    later_mod_ref = refs[1] if n_later else None
    qa_s, qb_s, k_s, vt_s, gt_s, ogt_s, w_in_t_s, w_out_s = refs[-8:]
    tm = ATTN_HALF_TILE
    batch = pl.program_id(0)
    j = pl.program_id(1)

    shift, scale, gate = _modulation(mod_ref, batch)
    g_norm = g_ref[layer:layer + 1, :]
    lane_tile = lambda a: jnp.concatenate([a] * (tm // LANES), axis=1)
    invf = lane_tile(small_ref[0:ROT_HALF, :])
    g0 = ROT_HALF + attn_layer * 2 * HEAD_DIM
    q_gain = lane_tile(small_ref[g0:g0 + HEAD_DIM, :])
    k_gain = lane_tile(small_ref[g0 + HEAD_DIM:g0 + 2 * HEAD_DIM, :])

    if n_later:
        _later_adaln_step(later, later_mod_ref, layer)

    @pl.when((pl.program_id(0) == 0) & (j == 0))
    def _():
        for c in range(ATTN_IN_WIDTH // LANES):
            cols = slice(c * LANES, (c + 1) * LANES)
            w_in_t_s[cols, :] = w_in_ref[0, :, cols].T.astype(jnp.bfloat16)
        for c in range(Q_WIDTH // LANES):
            rows = slice(c * LANES, (c + 1) * LANES)
            w_out_s[rows, :] = w_out_ref[0, rows, :].astype(jnp.bfloat16)

    @pl.when(j == 0)
    def _():
        k_s[0:BLOCK, :] = jnp.zeros((BLOCK, k_s.shape[1]), k_s.dtype)
        vt_s[:, 0:BLOCK] = jnp.zeros((vt_s.shape[0], BLOCK), vt_s.dtype)

    kq = lax.broadcasted_iota(jnp.int32, (SUB, 2 * BLOCK), 0)
    iq = lax.broadcasted_iota(jnp.int32, (SUB, 2 * BLOCK), 1) & (SUB - 1)
    use_new = kq <= iq
    prev_bias = jnp.where(j > 0, 0.0, -jnp.inf)
    lane4 = lax.broadcasted_iota(jnp.int32, (1, 2 * BLOCK), 1) // SUB
    low_q = (lax.broadcasted_iota(jnp.int32, (HEAD_DIM, tm), 1) & SUB) == 0
    low_o = lax.broadcasted_iota(jnp.int32, (HEAD_DIM, LANES), 1) < SUB
    zero = jnp.zeros((HEAD_DIM, LANES), jnp.bfloat16)
    zero_p = jnp.zeros((SUB, 2 * BLOCK), jnp.bfloat16)
    ones = jnp.ones((V_PAD, tm), jnp.bfloat16)

    def projection(hf):
        toks = slice(hf * tm, (hf + 1) * tm)
        krows = slice(BLOCK + hf * tm, BLOCK + (hf + 1) * tm)
        st = {}

        def norm():
            st["h"] = _modulated_norm(x_ref[0, toks, :], shift, scale, g_norm)
            ang = pos_ref[pl.ds(batch, 1), toks].astype(jnp.float32) * invf
            st["cos"], st["sin"] = jnp.cos(ang), jnp.sin(ang)

        def norm_rope(t, gain):
            ssq = jnp.sum(t * t, axis=0, keepdims=True)
            tn = t * lax.rsqrt(ssq * (1.0 / HEAD_DIM) + NORM_EPS) * gain
            x1, x2 = tn[0:ROT_HALF], tn[ROT_HALF:ROT_DIM]
            cos, sin = st["cos"], st["sin"]
            return jnp.concatenate([x1 * cos - x2 * sin, x2 * cos + x1 * sin, tn[ROT_DIM:]], axis=0)

        def chunk(c):
            row0 = c * PROJ_CHUNK
            pt = lax.dot_general(w_in_t_s[row0:row0 + PROJ_CHUNK, :], st["h"], _NT,
                                 preferred_element_type=jnp.float32)
            if row0 >= Q_WIDTH + 2 * KV_WIDTH:
                g0 = row0 - (Q_WIDTH + 2 * KV_WIDTH)
                gt_s[g0:g0 + PROJ_CHUNK, toks] = _silu(pt)
                return
            for i in range(PROJ_CHUNK // HEAD_DIM):
                t = pt[i * HEAD_DIM:(i + 1) * HEAD_DIM]
                hd = row0 // HEAD_DIM + i
                if hd < N_HEADS:
                    qn = norm_rope(t, q_gain)
                    keep = low_q if hd % 2 == 0 else ~low_q
                    swapped = jnp.concatenate(
                        [pltpu.roll(qn[:, c2 * LANES:(c2 + 1) * LANES], SUB, 1)
                         for c2 in range(tm // LANES)], axis=1)
                    hrows = slice(hd * HEAD_DIM, (hd + 1) * HEAD_DIM)
                    qa_s[hrows, toks] = jnp.where(keep, qn, 0.0).astype(jnp.bfloat16)
                    qb_s[hrows, toks] = jnp.where(keep, swapped, 0.0).astype(jnp.bfloat16)
                elif hd < N_HEADS + N_KV_HEADS:
                    kh = hd - N_HEADS
                    kt = norm_rope(t, k_gain)
                    k_s[krows, kh * LANES:(kh + 1) * LANES] = (
                        jnp.concatenate([kt, kt], axis=0).T.astype(jnp.bfloat16))
                else:
                    kh = hd - N_HEADS - N_KV_HEADS
                    vt_s[kh * V_ROWS:kh * V_ROWS + HEAD_DIM, krows] = t.astype(jnp.bfloat16)
                    vt_s[kh * V_ROWS + HEAD_DIM:(kh + 1) * V_ROWS, krows] = ones

        return [norm] + [(lambda c=c: chunk(c)) for c in range(ATTN_IN_WIDTH // PROJ_CHUNK)]

    def scores(qb):
        qcols = slice(qb * BLOCK, (qb + 1) * BLOCK)
        out = []
        for kh in range(N_KV_HEADS):
            tiles = []
            for hq in range(2):
                u = 2 * qb + hq
                kd = k_s[u * SUB:(u + 3) * SUB, kh * LANES:(kh + 1) * LANES]
                blocks = []
                for g in range(GROUP):
                    hrows = slice((kh * GROUP + g) * HEAD_DIM, (kh * GROUP + g + 1) * HEAD_DIM)
                    src = (qa_s if g % 2 == hq else qb_s)[hrows, qcols]
                    blocks.append(jnp.concatenate([src, zero] if g < 2 else [zero, src], axis=1))
                sc = jnp.dot(jnp.concatenate([kd, kd], axis=1), jnp.concatenate(blocks, axis=0),
                             preferred_element_type=jnp.float32)
                old, mid, new = sc[:SUB], sc[SUB:2 * SUB], sc[2 * SUB:]
                if qb == 0:
                    old = old + prev_bias
                    if hq == 0:
                        mid = mid + prev_bias
                tiles.append(jnp.concatenate([mid, jnp.where(use_new, new, old)], axis=0))
            out.append(tiles)
        return out

    def finish(qb, tiles_all):
        qcols = slice(qb * BLOCK, (qb + 1) * BLOCK)
        keys = slice(qb * BLOCK, (qb + 2) * BLOCK)
        for kh, tiles in enumerate(tiles_all):
            vt = vt_s[kh * V_ROWS:(kh + 1) * V_ROWS, keys]
            h0 = kh * GROUP
            sink = jnp.where(lane4 == 0, sink_ref[attn_layer, h0], jnp.where(
                lane4 == 1, sink_ref[attn_layer, h0 + 1], jnp.where(
                    lane4 == 2, sink_ref[attn_layer, h0 + 2],
                    sink_ref[attn_layer, h0 + 3]))) * LOG2E
            outs = []
            for hq, sc in enumerate(tiles):
                m = jnp.maximum(jnp.max(sc, axis=0, keepdims=True), sink)
                p = jnp.exp2(sc - m).astype(jnp.bfloat16)
                p_mid, p_fold = p[:SUB], p[SUB:]
                window = [jnp.where(use_new, zero_p, p_fold), p_mid, jnp.where(use_new, p_fold, zero_p)]
                p2 = jnp.concatenate(window + [zero_p] if hq == 0 else [zero_p] + window, axis=0)
                o = jnp.dot(vt, p2, preferred_element_type=jnp.float32)
                denom = o[HEAD_DIM:HEAD_DIM + 1] + jnp.exp2(sink - m)
                outs.append(o[:HEAD_DIM] * (1.0 / denom))
            for g in range(GROUP):
                par = g % 2
                cv = slice((g // 2) * LANES, (g // 2 + 1) * LANES)
                moved = pltpu.roll(outs[1 - par][:, cv], SUB, 1)
                full = jnp.where(low_o if par == 0 else ~low_o, outs[par][:, cv], moved)
                hrows = slice((h0 + g) * HEAD_DIM, (h0 + g + 1) * HEAD_DIM)
                ogt_s[hrows, qcols] = (full * gt_s[hrows, qcols]).astype(jnp.bfloat16)

    def attention(hf):
        blocks = range(hf * (tm // BLOCK), (hf + 1) * (tm // BLOCK))
        st = {}

        def score_step(qb):
            st[qb] = scores(qb)

        def finish_step(qb):
            finish(qb, st.pop(qb))

        steps = [lambda: score_step(blocks[0])]
        for qb in blocks:
            if qb + 1 in blocks:
                steps.append(lambda qb=qb: score_step(qb + 1))
            steps.append(lambda qb=qb: finish_step(qb))
        return steps

    def out_projection(hf):
        toks = slice(hf * tm, (hf + 1) * tm)

        def run():
            out = lax.dot_general(ogt_s[:, toks], w_out_s[...], _TN,
                                  preferred_element_type=jnp.float32)
            o_ref[0, toks, :] = x_ref[0, toks, :] + gate * out

        return [run]

    def interleave(a, b):
        done_a = done_b = 0
        while done_a < len(a) or done_b < len(b):
            if done_b >= len(b) or (done_a < len(a) and done_a * len(b) <= done_b * len(a)):
                a[done_a]()
                done_a += 1
            else:
                b[done_b]()
                done_b += 1

    for f in projection(0):
        f()
    for part in range(1, ATTN_PARTS):
        proj = projection(part)
        proj[0]()
        done = out_projection(part - 2) if part >= 2 else []
        interleave(attention(part - 1) + done, proj[1:])
    done = out_projection(ATTN_PARTS - 2) if ATTN_PARTS >= 2 else []
    interleave(attention(ATTN_PARTS - 1), done)
    out_projection(ATTN_PARTS - 1)[0]()

    step_tokens = ATTN_PARTS * tm
    k_s[0:BLOCK, :] = k_s[step_tokens:step_tokens + BLOCK, :]
    vt_s[:, 0:BLOCK] = vt_s[:, step_tokens:step_tokens + BLOCK]


def _attn_small_params(q_norm, k_norm):
    inv_freq = ROPE_THETA ** (-jnp.arange(ROT_HALF, dtype=jnp.float32) * 2.0 / ROT_DIM)
    cols = [inv_freq]
    for l in range(q_norm.shape[0]):
        cols += [q_norm[l] * (HEAD_DIM ** -0.5 * LOG2E), k_norm[l]]
    col = jnp.concatenate(cols)
    return jnp.broadcast_to(col[:, None], (col.shape[0], LANES))


def _attn_layer(x, mod_all, mod_idx, norm_g, positions, small, w_in_all, sinks, w_out_all,
                layer, attn_layer, later_adaln=None):
    b, s, d = x.shape
    tm = ATTN_HALF_TILE
    step = ATTN_PARTS * tm
    steps_per_seq = s // step
    const = lambda i, j: (0, 0)
    later_specs, later_args, later_out_specs, later_out_shape, n_later = _later_adaln_operands(
        later_adaln, layer, b, d, steps_per_seq)
    out_specs = [pl.BlockSpec((1, step, d), lambda i, j: (i, j, 0))] + later_out_specs
    out_shape = [jax.ShapeDtypeStruct(x.shape, x.dtype)] + later_out_shape
    return pl.pallas_call(
        functools.partial(_attn_kernel, n_later=n_later, layer=layer, attn_layer=attn_layer),
        grid=(b, steps_per_seq),
        in_specs=[
            pl.BlockSpec((1, step, d), lambda i, j: (i, j, 0)),
            pl.BlockSpec((1, b, 3 * d), lambda i, j: (mod_idx, 0, 0)),
            pl.BlockSpec(norm_g.shape, const),
            pl.BlockSpec((b, step), lambda i, j: (0, j)),
            pl.BlockSpec(small.shape, const),
            pl.BlockSpec((1, d, ATTN_IN_WIDTH), lambda i, j: (attn_layer, 0, 0)),
            pl.BlockSpec((1, Q_WIDTH, d), lambda i, j: (attn_layer, 0, 0)),
            pl.BlockSpec(memory_space=pltpu.SMEM),
        ] + later_specs,
        out_specs=out_specs,
        out_shape=out_shape,
        scratch_shapes=[
            pltpu.VMEM((Q_WIDTH, step), jnp.bfloat16),
            pltpu.VMEM((Q_WIDTH, step), jnp.bfloat16),
            pltpu.VMEM((BLOCK + step, N_KV_HEADS * LANES), jnp.bfloat16),
            pltpu.VMEM((N_KV_HEADS * V_ROWS, BLOCK + step), jnp.bfloat16),
            pltpu.VMEM((Q_WIDTH, step), jnp.float32),
            pltpu.VMEM((Q_WIDTH, step), jnp.bfloat16),
            pltpu.VMEM((ATTN_IN_WIDTH, d), jnp.bfloat16),
            pltpu.VMEM((Q_WIDTH, d), jnp.bfloat16),
        ],
        compiler_params=pltpu.CompilerParams(
            dimension_semantics=("arbitrary", "arbitrary"), vmem_limit_bytes=VMEM_LIMIT),
        name="attn_layer",
    )(x, mod_all, norm_g, positions, small, w_in_all, w_out_all, sinks, *later_args)


def _pool_kernel(*refs, n_later, layer, pool_layer):
    n_in = 7
    x_ref, mod_ref, g_ref, w_in_ref, w_grp_ref, scale_ref, w_out_ref = refs[:n_in]
    later = refs[n_in:n_in + (n_later + 2 if n_later else 0)]
    o_ref = refs[n_in + len(later)]
    v_s, w_in_s, w_grp_s, w_out_s = refs[-4:]
    tm = POOL_PART_TILE
    j = pl.program_id(1)
    if n_later:
        _later_adaln_step(later, refs[n_in + len(later) + 1], layer)
    shift, scale, gate = _modulation(mod_ref, pl.program_id(0))
    g_norm = g_ref[layer:layer + 1, :]
    mix_scale = scale_ref[pool_layer:pool_layer + 1, :]

    @pl.when((pl.program_id(0) == 0) & (j == 0))
    def _():
        for c in range(D_MODEL // LANES):
            rows = slice(c * LANES, (c + 1) * LANES)
            w_in_s[rows, :] = w_in_ref[0, rows, :].astype(jnp.bfloat16)
            w_out_s[rows, :] = w_out_ref[0, rows, :].astype(jnp.bfloat16)
        for gi in range(len(POOL_WINDOWS)):
            w_grp_s[gi] = w_grp_ref[0, gi].astype(jnp.bfloat16)

    @pl.when(j == 0)
    def _():
        v_s[0:POOL_HALO, :] = jnp.zeros((POOL_HALO, v_s.shape[1]), v_s.dtype)

    gates = {}

    def projection(hf):
        toks = slice(hf * tm, (hf + 1) * tm)
        st = {}

        def norm():
            st["h"] = _modulated_norm(x_ref[0, toks, :], shift, scale, g_norm)

        def value_path():
            v_s[POOL_HALO + hf * tm:POOL_HALO + (hf + 1) * tm, :] = jnp.dot(
                st["h"], w_in_s[:, :D_MODEL], preferred_element_type=jnp.float32)

        def gate_path():
            gates[hf] = _silu(jnp.dot(st["h"], w_in_s[:, D_MODEL:],
                                      preferred_element_type=jnp.float32))

        return [norm, value_path, gate_path]

    def mixing(hf):
        toks = slice(hf * tm, (hf + 1) * tm)
        ext = slice(hf * tm, POOL_HALO + (hf + 1) * tm)
        t = lax.broadcasted_iota(jnp.int32, (tm, POOL_GROUP_DIM), 0) + (j * POOL_PARTS + hf) * tm
        mixed = []

        def group(gi):
            w = POOL_WINDOWS[gi]
            cols = slice(gi * POOL_GROUP_DIM, (gi + 1) * POOL_GROUP_DIM)
            win = v_s[ext, cols]
            cur = win[POOL_HALO:]
            span = 1
            while span < w:
                win = win + pltpu.roll(win, span, 0)
                span *= 2
            count = jnp.minimum(t + 1, w).astype(jnp.float32)
            pooled = win[POOL_HALO:] / count - cur
            mixed.append(jnp.dot(pooled.astype(jnp.bfloat16), w_grp_s[gi],
                                 preferred_element_type=jnp.float32))

        def out():
            m = jnp.concatenate(mixed, axis=1) * mix_scale
            o = jnp.dot((m * gates[hf]).astype(jnp.bfloat16), w_out_s[...],
                        preferred_element_type=jnp.float32)
            o_ref[0, toks, :] = x_ref[0, toks, :] + gate * o

        return [(lambda gi=gi: group(gi)) for gi in range(len(POOL_WINDOWS))] + [out]

    def interleave(a, b):
        done_a = done_b = 0
        while done_a < len(a) or done_b < len(b):
            if done_b >= len(b) or (done_a < len(a) and done_a * len(b) <= done_b * len(a)):
                a[done_a]()
                done_a += 1
            else:
                b[done_b]()
                done_b += 1

    for f in projection(0):
        f()
    for part in range(1, POOL_PARTS):
        proj = projection(part)
        proj[0]()
        interleave(mixing(part - 1), proj[1:])
    for f in mixing(POOL_PARTS - 1):
        f()

    v_s[0:POOL_HALO, :] = v_s[POOL_PARTS * tm:POOL_PARTS * tm + POOL_HALO, :]


def _pool_layer(x, mod_all, mod_idx, norm_g, w_in_all, w_group_all, scale_all, w_out_all,
                layer, pool_layer, later_adaln=None):
    b, s, d = x.shape
    tm = POOL_PARTS * POOL_PART_TILE
    n_grp = len(POOL_WINDOWS)
    const = lambda i, j: (0, 0)
    later_specs, later_args, later_out_specs, later_out_shape, n_later = _later_adaln_operands(
        later_adaln, layer, b, d, s // tm)
    return pl.pallas_call(
        functools.partial(_pool_kernel, n_later=n_later, layer=layer, pool_layer=pool_layer),
        grid=(b, s // tm),
        in_specs=[
            pl.BlockSpec((1, tm, d), lambda i, j: (i, j, 0)),
            pl.BlockSpec((1, b, 3 * d), lambda i, j: (mod_idx, 0, 0)),
            pl.BlockSpec(norm_g.shape, const),
            pl.BlockSpec((1, d, 2 * D_MODEL), lambda i, j: (pool_layer, 0, 0)),
            pl.BlockSpec((1, n_grp, POOL_GROUP_DIM, POOL_GROUP_DIM), lambda i, j: (pool_layer, 0, 0, 0)),
            pl.BlockSpec(scale_all.shape, const),
            pl.BlockSpec((1, D_MODEL, d), lambda i, j: (pool_layer, 0, 0)),
        ] + later_specs,
        out_specs=[pl.BlockSpec((1, tm, d), lambda i, j: (i, j, 0))] + later_out_specs,
        out_shape=[jax.ShapeDtypeStruct(x.shape, x.dtype)] + later_out_shape,
        scratch_shapes=[
            pltpu.VMEM((POOL_HALO + tm, D_MODEL), jnp.float32),
            pltpu.VMEM((d, 2 * D_MODEL), jnp.bfloat16),
            pltpu.VMEM((n_grp, POOL_GROUP_DIM, POOL_GROUP_DIM), jnp.bfloat16),
            pltpu.VMEM((D_MODEL, d), jnp.bfloat16),
        ],
        compiler_params=pltpu.CompilerParams(
            dimension_semantics=("arbitrary", "arbitrary"), vmem_limit_bytes=VMEM_LIMIT),
        name="pool_layer",
    )(x, mod_all, norm_g, w_in_all, w_group_all, scale_all, w_out_all, *later_args)


def kernel(x, c, positions, ada_w, ada_b, norm_g, attn_w_in, attn_q_norm, attn_k_norm, attn_sinks,
           attn_w_out, pool_w_in, pool_w_group, pool_scale, pool_w_out):
    depth = ada_w.shape[0]
    small = _attn_small_params(attn_q_norm, attn_k_norm)
    n_front = min(ADALN_UPFRONT, depth)
    mod_front = _adaln(c, ada_w, ada_b, n_front)
    mods = [(mod_front, l) for l in range(n_front)]
    for i in range(depth):
        jl = i // 2
        mod_all, mod_idx = mods[i]
        later = (c, ada_w, ada_b) if (i == n_front - 1 and depth > n_front) else None
        if i % 2 == 0:
            outs = _attn_layer(x, mod_all, mod_idx, norm_g, positions, small, attn_w_in, attn_sinks,
                               attn_w_out, i, jl, later)
        else:
            outs = _pool_layer(x, mod_all, mod_idx, norm_g, pool_w_in, pool_w_group, pool_scale,
                               pool_w_out, i, jl, later)
        x = outs[0]
        if later is not None:
            mods += [(outs[1], l) for l in range(depth - n_front)]
    return x
```

```python
import functools
import math

import jax
import jax.numpy as jnp
from jax import lax
from jax.experimental import pallas as pl
from jax.experimental.pallas import tpu as pltpu

D_MODEL = 1024
HEAD_DIM = 64
N_HEADS = 16
N_KV_HEADS = 4
GROUP = N_HEADS // N_KV_HEADS
Q_WIDTH = N_HEADS * HEAD_DIM
KV_WIDTH = N_KV_HEADS * HEAD_DIM
ATTN_IN_WIDTH = 2 * Q_WIDTH + 2 * KV_WIDTH
BLOCK = 128
SUB = BLOCK // 2
ROT_DIM = HEAD_DIM // 4
ROT_HALF = ROT_DIM // 2
ROPE_THETA = 500000.0
POOL_WINDOWS = (2, 4, 8, 16)
POOL_GROUP_DIM = D_MODEL // len(POOL_WINDOWS)
POOL_HALO = 16
NORM_EPS = 1e-6
LANES = 128
BF16_ROWS = 16
V_PAD = BF16_ROWS
V_ROWS = HEAD_DIM + V_PAD
LOG2E = math.log2(math.e)

ATTN_HALF_TILE = 256
ATTN_PARTS = 4
PROJ_CHUNK = 512
POOL_PART_TILE = 512
POOL_PARTS = 2
POOL_PROJ_CHUNK = 256
ADALN_UPFRONT = 2
VMEM_LIMIT = 60 * 1024 * 1024

_NT = (((1,), (1,)), ((), ()))
_TN = (((0,), (0,)), ((), ()))


def _silu(x):
    hx = 0.5 * x
    return hx + hx * jnp.tanh(hx)


def _adaln_kernel(c_ref, w_ref, b_ref, o_ref):
    a = _silu(c_ref[...]).astype(jnp.bfloat16)
    w = w_ref[0].astype(jnp.bfloat16)
    bias = b_ref[pl.ds(pl.program_id(0), 1), :]
    o_ref[0] = jnp.dot(a, w, preferred_element_type=jnp.float32) + bias


def _adaln(c, ada_w, ada_b, depth):
    _, d, n = ada_w.shape
    b = c.shape[0]
    tn = 1024
    return pl.pallas_call(
        _adaln_kernel,
        grid=(depth, n // tn),
        in_specs=[
            pl.BlockSpec((b, d), lambda l, j: (0, 0)),
            pl.BlockSpec((1, d, tn), lambda l, j: (l, 0, j)),
            pl.BlockSpec((ada_b.shape[0], tn), lambda l, j: (0, j)),
        ],
        out_specs=pl.BlockSpec((1, b, tn), lambda l, j: (l, 0, j)),
        out_shape=jax.ShapeDtypeStruct((depth, b, n), jnp.float32),
        compiler_params=pltpu.CompilerParams(
            dimension_semantics=("arbitrary", "arbitrary"), vmem_limit_bytes=VMEM_LIMIT),
        name="adaln_mod",
    )(c, ada_w, ada_b)


def _modulation(mod_ref, batch):
    row = mod_ref[0, pl.ds(batch, 1), :]
    return row[:, :D_MODEL], row[:, D_MODEL:2 * D_MODEL], row[:, 2 * D_MODEL:]


def _modulated_norm(x, shift, scale, g):
    ms = jnp.mean(x * x, axis=-1, keepdims=True)
    gain = g * (1.0 + scale)
    h = x * lax.rsqrt(ms + NORM_EPS) * gain + shift
    return h.astype(jnp.bfloat16)


def _later_adaln_step(later, later_mod_ref, layer):
    c_ref, w_refs, b_ref = later[0], later[1:-1], later[-1]

    @pl.when((pl.program_id(0) == 0) & (pl.program_id(1) == 0))
    def _():
        for l in range(len(w_refs)):
            later_mod_ref[l] = jnp.broadcast_to(b_ref[layer + 1 + l:layer + 2 + l, :],
                                                later_mod_ref.shape[1:])

    a = _silu(c_ref[0]).astype(jnp.bfloat16)
    for l, w_ref in enumerate(w_refs):
        later_mod_ref[l] += jnp.dot(a, w_ref[0].astype(jnp.bfloat16),
                                    preferred_element_type=jnp.float32)


def _later_adaln_operands(later_adaln, layer, b, d, steps_per_seq):
    if later_adaln is None:
        return [], [], [], [], 0
    c, ada_w, ada_b = later_adaln
    first = layer + 1
    n_later = ada_w.shape[0] - first
    n_steps = b * steps_per_seq
    rows = d // n_steps
    assert rows * n_steps == d and rows % 8 == 0
    flat = lambda i, j: i * steps_per_seq + j
    specs = ([pl.BlockSpec((1, b, rows), lambda i, j: (flat(i, j), 0, 0))]
             + [pl.BlockSpec((1, rows, 3 * d), lambda i, j, l=l: (first + l, flat(i, j), 0))
                for l in range(n_later)]
             + [pl.BlockSpec(ada_b.shape, lambda i, j: (0, 0))])
    args = [c.reshape(b, n_steps, rows).transpose(1, 0, 2)] + [ada_w] * n_later + [ada_b]
    out_specs = [pl.BlockSpec((n_later, b, 3 * d), lambda i, j: (0, 0, 0))]
    out_shape = [jax.ShapeDtypeStruct((n_later, b, 3 * d), jnp.float32)]
    return specs, args, out_specs, out_shape, n_later


def _attn_kernel(*refs, n_later, layer, attn_layer):
    n_in = 8
    x_ref, mod_ref, g_ref, pos_ref, small_ref, w_in_ref, w_out_ref, sink_ref = refs[:n_in]
    later = refs[n_in:n_in + (n_later + 2 if n_later else 0)]
    refs = refs[n_in + len(later):]
    o_ref = refs[0]
    later_mod_ref = refs[1] if n_later else None
    qa_s, qb_s, k_s, vt_s, gt_s, ogt_s, w_in_t_s, w_out_s = refs[-8:]
    tm = ATTN_HALF_TILE
    batch = pl.program_id(0)
    j = pl.program_id(1)

    shift, scale, gate = _modulation(mod_ref, batch)
    g_norm = g_ref[layer:layer + 1, :]
    lane_tile = lambda a: jnp.concatenate([a] * (tm // LANES), axis=1)
    invf = lane_tile(small_ref[0:ROT_HALF, :])
    g0 = ROT_HALF + attn_layer * 2 * HEAD_DIM
    q_gain = lane_tile(small_ref[g0:g0 + HEAD_DIM, :])
    k_gain = lane_tile(small_ref[g0 + HEAD_DIM:g0 + 2 * HEAD_DIM, :])

    if n_later:
        _later_adaln_step(later, later_mod_ref, layer)

    @pl.when((pl.program_id(0) == 0) & (j == 0))
    def _():
        for c in range(ATTN_IN_WIDTH // LANES):
            cols = slice(c * LANES, (c + 1) * LANES)
            w_in_t_s[cols, :] = w_in_ref[0, :, cols].T.astype(jnp.bfloat16)
        for c in range(Q_WIDTH // LANES):
            rows = slice(c * LANES, (c + 1) * LANES)
            w_out_s[rows, :] = w_out_ref[0, rows, :].astype(jnp.bfloat16)

    @pl.when(j == 0)
    def _():
        k_s[0:BLOCK, :] = jnp.zeros((BLOCK, k_s.shape[1]), k_s.dtype)
        vt_s[:, 0:BLOCK] = jnp.zeros((vt_s.shape[0], BLOCK), vt_s.dtype)

    kq = lax.broadcasted_iota(jnp.int32, (SUB, 2 * BLOCK), 0)
    iq = lax.broadcasted_iota(jnp.int32, (SUB, 2 * BLOCK), 1) & (SUB - 1)
    use_new = kq <= iq
    prev_bias = jnp.where(j > 0, 0.0, -jnp.inf)
    lane4 = lax.broadcasted_iota(jnp.int32, (1, 2 * BLOCK), 1) // SUB
    low_q = (lax.broadcasted_iota(jnp.int32, (HEAD_DIM, tm), 1) & SUB) == 0
    low_o = lax.broadcasted_iota(jnp.int32, (HEAD_DIM, LANES), 1) < SUB
    zero = jnp.zeros((HEAD_DIM, LANES), jnp.bfloat16)
    zero_p = jnp.zeros((SUB, 2 * BLOCK), jnp.bfloat16)
    ones = jnp.ones((V_PAD, tm), jnp.bfloat16)

    def projection(hf):
        toks = slice(hf * tm, (hf + 1) * tm)
        krows = slice(BLOCK + hf * tm, BLOCK + (hf + 1) * tm)
        st = {}

        def norm():
            st["h"] = _modulated_norm(x_ref[0, toks, :], shift, scale, g_norm)
            ang = pos_ref[pl.ds(batch, 1), toks].astype(jnp.float32) * invf
            st["cos"], st["sin"] = jnp.cos(ang), jnp.sin(ang)

        def norm_rope(t, gain):
            ssq = jnp.sum(t * t, axis=0, keepdims=True)
            tn = t * lax.rsqrt(ssq * (1.0 / HEAD_DIM) + NORM_EPS) * gain
            x1, x2 = tn[0:ROT_HALF], tn[ROT_HALF:ROT_DIM]
            cos, sin = st["cos"], st["sin"]
            return jnp.concatenate([x1 * cos - x2 * sin, x2 * cos + x1 * sin, tn[ROT_DIM:]], axis=0)

        def chunk(c):
            row0 = c * PROJ_CHUNK
            pt = lax.dot_general(w_in_t_s[row0:row0 + PROJ_CHUNK, :], st["h"], _NT,
                                 preferred_element_type=jnp.float32)
            if row0 >= Q_WIDTH + 2 * KV_WIDTH:
                g0 = row0 - (Q_WIDTH + 2 * KV_WIDTH)
                gt_s[g0:g0 + PROJ_CHUNK, toks] = _silu(pt)
                return
            for i in range(PROJ_CHUNK // HEAD_DIM):
                t = pt[i * HEAD_DIM:(i + 1) * HEAD_DIM]
                hd = row0 // HEAD_DIM + i
                if hd < N_HEADS:
                    qn = norm_rope(t, q_gain)
                    keep = low_q if hd % 2 == 0 else ~low_q
                    swapped = jnp.concatenate(
                        [pltpu.roll(qn[:, c2 * LANES:(c2 + 1) * LANES], SUB, 1)
                         for c2 in range(tm // LANES)], axis=1)
                    hrows = slice(hd * HEAD_DIM, (hd + 1) * HEAD_DIM)
                    qa_s[hrows, toks] = jnp.where(keep, qn, 0.0).astype(jnp.bfloat16)
                    qb_s[hrows, toks] = jnp.where(keep, swapped, 0.0).astype(jnp.bfloat16)
                elif hd < N_HEADS + N_KV_HEADS:
                    kh = hd - N_HEADS
                    kt = norm_rope(t, k_gain)
                    k_s[krows, kh * LANES:(kh + 1) * LANES] = (
                        jnp.concatenate([kt, kt], axis=0).T.astype(jnp.bfloat16))
                else:
                    kh = hd - N_HEADS - N_KV_HEADS
                    vt_s[kh * V_ROWS:kh * V_ROWS + HEAD_DIM, krows] = t.astype(jnp.bfloat16)
                    vt_s[kh * V_ROWS + HEAD_DIM:(kh + 1) * V_ROWS, krows] = ones

        return [norm] + [(lambda c=c: chunk(c)) for c in range(ATTN_IN_WIDTH // PROJ_CHUNK)]

    def scores(qb):
        qcols = slice(qb * BLOCK, (qb + 1) * BLOCK)
        out = []
        for kh in range(N_KV_HEADS):
            tiles = []
            for hq in range(2):
                u = 2 * qb + hq
                kd = k_s[u * SUB:(u + 3) * SUB, kh * LANES:(kh + 1) * LANES]
                blocks = []
                for g in range(GROUP):
                    hrows = slice((kh * GROUP + g) * HEAD_DIM, (kh * GROUP + g + 1) * HEAD_DIM)
                    src = (qa_s if g % 2 == hq else qb_s)[hrows, qcols]
                    blocks.append(jnp.concatenate([src, zero] if g < 2 else [zero, src], axis=1))
                sc = jnp.dot(jnp.concatenate([kd, kd], axis=1), jnp.concatenate(blocks, axis=0),
                             preferred_element_type=jnp.float32)
                old, mid, new = sc[:SUB], sc[SUB:2 * SUB], sc[2 * SUB:]
                if qb == 0:
                    old = old + prev_bias
                    if hq == 0:
                        mid = mid + prev_bias
                tiles.append(jnp.concatenate([mid, jnp.where(use_new, new, old)], axis=0))
            out.append(tiles)
        return out

    def finish(qb, tiles_all):
        qcols = slice(qb * BLOCK, (qb + 1) * BLOCK)
        keys = slice(qb * BLOCK, (qb + 2) * BLOCK)
        for kh, tiles in enumerate(tiles_all):
            vt = vt_s[kh * V_ROWS:(kh + 1) * V_ROWS, keys]
            h0 = kh * GROUP
            sink = jnp.where(lane4 == 0, sink_ref[attn_layer, h0], jnp.where(
                lane4 == 1, sink_ref[attn_layer, h0 + 1], jnp.where(
                    lane4 == 2, sink_ref[attn_layer, h0 + 2],
                    sink_ref[attn_layer, h0 + 3]))) * LOG2E
            outs = []
            for hq, sc in enumerate(tiles):
                m = jnp.maximum(jnp.max(sc, axis=0, keepdims=True), sink)
                p = jnp.exp2(sc - m).astype(jnp.bfloat16)
                p_mid, p_fold = p[:SUB], p[SUB:]
                window = [jnp.where(use_new, zero_p, p_fold), p_mid, jnp.where(use_new, p_fold, zero_p)]
                p2 = jnp.concatenate(window + [zero_p] if hq == 0 else [zero_p] + window, axis=0)
                o = jnp.dot(vt, p2, preferred_element_type=jnp.float32)
                denom = o[HEAD_DIM:HEAD_DIM + 1] + jnp.exp2(sink - m)
                outs.append(o[:HEAD_DIM] * (1.0 / denom))
            for g in range(GROUP):
                par = g % 2
                cv = slice((g // 2) * LANES, (g // 2 + 1) * LANES)
                moved = pltpu.roll(outs[1 - par][:, cv], SUB, 1)
                full = jnp.where(low_o if par == 0 else ~low_o, outs[par][:, cv], moved)
                hrows = slice((h0 + g) * HEAD_DIM, (h0 + g + 1) * HEAD_DIM)
                ogt_s[hrows, qcols] = (full * gt_s[hrows, qcols]).astype(jnp.bfloat16)

    def attention(hf):
        blocks = range(hf * (tm // BLOCK), (hf + 1) * (tm // BLOCK))
        st = {}

        def score_step(qb):
            st[qb] = scores(qb)

        def finish_step(qb):
            finish(qb, st.pop(qb))

        steps = [lambda: score_step(blocks[0])]
        for qb in blocks:
            if qb + 1 in blocks:
                steps.append(lambda qb=qb: score_step(qb + 1))
            steps.append(lambda qb=qb: finish_step(qb))
        return steps

    def out_projection(hf):
        toks = slice(hf * tm, (hf + 1) * tm)

        def run():
            out = lax.dot_general(ogt_s[:, toks], w_out_s[...], _TN,
                                  preferred_element_type=jnp.float32)
            o_ref[0, toks, :] = x_ref[0, toks, :] + gate * out

        return [run]

    def interleave(a, b):
        done_a = done_b = 0
        while done_a < len(a) or done_b < len(b):
            if done_b >= len(b) or (done_a < len(a) and done_a * len(b) <= done_b * len(a)):
                a[done_a]()
                done_a += 1
            else:
                b[done_b]()
                done_b += 1

    for f in projection(0):
        f()
    for part in range(1, ATTN_PARTS):
        proj = projection(part)
        proj[0]()
        done = out_projection(part - 2) if part >= 2 else []
        interleave(attention(part - 1) + done, proj[1:])
    done = out_projection(ATTN_PARTS - 2) if ATTN_PARTS >= 2 else []
    interleave(attention(ATTN_PARTS - 1), done)
    out_projection(ATTN_PARTS - 1)[0]()

    step_tokens = ATTN_PARTS * tm
    k_s[0:BLOCK, :] = k_s[step_tokens:step_tokens + BLOCK, :]
    vt_s[:, 0:BLOCK] = vt_s[:, step_tokens:step_tokens + BLOCK]


def _attn_small_params(q_norm, k_norm):
    inv_freq = ROPE_THETA ** (-jnp.arange(ROT_HALF, dtype=jnp.float32) * 2.0 / ROT_DIM)
    cols = [inv_freq]
    for l in range(q_norm.shape[0]):
        cols += [q_norm[l] * (HEAD_DIM ** -0.5 * LOG2E), k_norm[l]]
    col = jnp.concatenate(cols)
    return jnp.broadcast_to(col[:, None], (col.shape[0], LANES))


def _attn_layer(x, mod_all, mod_idx, norm_g, positions, small, w_in_all, sinks, w_out_all,
                layer, attn_layer, later_adaln=None):
    b, s, d = x.shape
    tm = ATTN_HALF_TILE
    step = ATTN_PARTS * tm
    steps_per_seq = s // step
    const = lambda i, j: (0, 0)
    later_specs, later_args, later_out_specs, later_out_shape, n_later = _later_adaln_operands(
        later_adaln, layer, b, d, steps_per_seq)
    out_specs = [pl.BlockSpec((1, step, d), lambda i, j: (i, j, 0))] + later_out_specs
    out_shape = [jax.ShapeDtypeStruct(x.shape, x.dtype)] + later_out_shape
    return pl.pallas_call(
        functools.partial(_attn_kernel, n_later=n_later, layer=layer, attn_layer=attn_layer),
        grid=(b, steps_per_seq),
        in_specs=[
            pl.BlockSpec((1, step, d), lambda i, j: (i, j, 0)),
            pl.BlockSpec((1, b, 3 * d), lambda i, j: (mod_idx, 0, 0)),
            pl.BlockSpec(norm_g.shape, const),
            pl.BlockSpec((b, step), lambda i, j: (0, j)),
            pl.BlockSpec(small.shape, const),
            pl.BlockSpec((1, d, ATTN_IN_WIDTH), lambda i, j: (attn_layer, 0, 0)),
            pl.BlockSpec((1, Q_WIDTH, d), lambda i, j: (attn_layer, 0, 0)),
            pl.BlockSpec(memory_space=pltpu.SMEM),
        ] + later_specs,
        out_specs=out_specs,
        out_shape=out_shape,
        scratch_shapes=[
            pltpu.VMEM((Q_WIDTH, step), jnp.bfloat16),
            pltpu.VMEM((Q_WIDTH, step), jnp.bfloat16),
            pltpu.VMEM((BLOCK + step, N_KV_HEADS * LANES), jnp.bfloat16),
            pltpu.VMEM((N_KV_HEADS * V_ROWS, BLOCK + step), jnp.bfloat16),
            pltpu.VMEM((Q_WIDTH, step), jnp.float32),
            pltpu.VMEM((Q_WIDTH, step), jnp.bfloat16),
            pltpu.VMEM((ATTN_IN_WIDTH, d), jnp.bfloat16),
            pltpu.VMEM((Q_WIDTH, d), jnp.bfloat16),
        ],
        compiler_params=pltpu.CompilerParams(
            dimension_semantics=("arbitrary", "arbitrary"), vmem_limit_bytes=VMEM_LIMIT),
        name="attn_layer",
    )(x, mod_all, norm_g, positions, small, w_in_all, w_out_all, sinks, *later_args)


def _pool_kernel(*refs, n_later, layer, pool_layer):
    n_in = 7
    x_ref, mod_ref, g_ref, w_in_ref, w_grp_ref, scale_ref, w_out_ref = refs[:n_in]
    later = refs[n_in:n_in + (n_later + 2 if n_later else 0)]
    o_ref = refs[n_in + len(later)]
    v_s, w_in_s, w_grp_s, w_out_s = refs[-4:]
    tm = POOL_PART_TILE
    j = pl.program_id(1)
    if n_later:
        _later_adaln_step(later, refs[n_in + len(later) + 1], layer)
    shift, scale, gate = _modulation(mod_ref, pl.program_id(0))
    g_norm = g_ref[layer:layer + 1, :]
    mix_scale = scale_ref[pool_layer:pool_layer + 1, :]

    @pl.when((pl.program_id(0) == 0) & (j == 0))
    def _():
        for c in range(D_MODEL // LANES):
            rows = slice(c * LANES, (c + 1) * LANES)
            w_in_s[rows, :] = w_in_ref[0, rows, :].astype(jnp.bfloat16)
            w_out_s[rows, :] = w_out_ref[0, rows, :].astype(jnp.bfloat16)
        for gi in range(len(POOL_WINDOWS)):
            w_grp_s[gi] = w_grp_ref[0, gi].astype(jnp.bfloat16)

    @pl.when(j == 0)
    def _():
        v_s[0:POOL_HALO, :] = jnp.zeros((POOL_HALO, v_s.shape[1]), v_s.dtype)

    gates = {}

    def projection(hf):
        toks = slice(hf * tm, (hf + 1) * tm)
        st = {}

        def norm():
            st["h"] = _modulated_norm(x_ref[0, toks, :], shift, scale, g_norm)

        def value_path(cq):
            cols = slice(cq * POOL_PROJ_CHUNK, (cq + 1) * POOL_PROJ_CHUNK)
            v_s[POOL_HALO + hf * tm:POOL_HALO + (hf + 1) * tm, cols] = jnp.dot(
                st["h"], w_in_s[:, cols], preferred_element_type=jnp.float32)

        def gate_path(cq):
            cols = slice(D_MODEL + cq * POOL_PROJ_CHUNK, D_MODEL + (cq + 1) * POOL_PROJ_CHUNK)
            gates[hf, cq] = _silu(jnp.dot(st["h"], w_in_s[:, cols],
                                          preferred_element_type=jnp.float32))

        n_chunks = D_MODEL // POOL_PROJ_CHUNK
        return ([norm] + [(lambda cq=cq: value_path(cq)) for cq in range(n_chunks)]
                + [(lambda cq=cq: gate_path(cq)) for cq in range(n_chunks)])

    def mixing(hf):
        toks = slice(hf * tm, (hf + 1) * tm)
        ext = slice(hf * tm, POOL_HALO + (hf + 1) * tm)
        t = lax.broadcasted_iota(jnp.int32, (tm, POOL_GROUP_DIM), 0) + (j * POOL_PARTS + hf) * tm
        mixed = []

        def group(gi):
            w = POOL_WINDOWS[gi]
            cols = slice(gi * POOL_GROUP_DIM, (gi + 1) * POOL_GROUP_DIM)
            win = v_s[ext, cols]
            cur = win[POOL_HALO:]
            span = 1
            while span < w:
                win = win + pltpu.roll(win, span, 0)
                span *= 2
            count = jnp.minimum(t + 1, w).astype(jnp.float32)
            pooled = win[POOL_HALO:] / count - cur
            mixed.append(jnp.dot(pooled.astype(jnp.bfloat16), w_grp_s[gi],
                                 preferred_element_type=jnp.float32))

        def out():
            m = jnp.concatenate(mixed, axis=1) * mix_scale
            g_all = jnp.concatenate(
                [gates[hf, cq] for cq in range(D_MODEL // POOL_PROJ_CHUNK)], axis=1)
            o = jnp.dot((m * g_all).astype(jnp.bfloat16), w_out_s[...],
                        preferred_element_type=jnp.float32)
            o_ref[0, toks, :] = x_ref[0, toks, :] + gate * o

        return [(lambda gi=gi: group(gi)) for gi in range(len(POOL_WINDOWS))] + [out]

    def interleave(a, b):
        done_a = done_b = 0
        while done_a < len(a) or done_b < len(b):
            if done_b >= len(b) or (done_a < len(a) and done_a * len(b) <= done_b * len(a)):
                a[done_a]()
                done_a += 1
            else:
                b[done_b]()
                done_b += 1

    for f in projection(0):
        f()
    for part in range(1, POOL_PARTS):
        proj = projection(part)
        proj[0]()
        interleave(mixing(part - 1), proj[1:])
    for f in mixing(POOL_PARTS - 1):
        f()

    v_s[0:POOL_HALO, :] = v_s[POOL_PARTS * tm:POOL_PARTS * tm + POOL_HALO, :]


def _pool_layer(x, mod_all, mod_idx, norm_g, w_in_all, w_group_all, scale_all, w_out_all,
                layer, pool_layer, later_adaln=None):
    b, s, d = x.shape
    tm = POOL_PARTS * POOL_PART_TILE
    n_grp = len(POOL_WINDOWS)
    const = lambda i, j: (0, 0)
    later_specs, later_args, later_out_specs, later_out_shape, n_later = _later_adaln_operands(
        later_adaln, layer, b, d, s // tm)
    return pl.pallas_call(
        functools.partial(_pool_kernel, n_later=n_later, layer=layer, pool_layer=pool_layer),
        grid=(b, s // tm),
        in_specs=[
            pl.BlockSpec((1, tm, d), lambda i, j: (i, j, 0)),
            pl.BlockSpec((1, b, 3 * d), lambda i, j: (mod_idx, 0, 0)),
            pl.BlockSpec(norm_g.shape, const),
            pl.BlockSpec((1, d, 2 * D_MODEL), lambda i, j: (pool_layer, 0, 0)),
            pl.BlockSpec((1, n_grp, POOL_GROUP_DIM, POOL_GROUP_DIM), lambda i, j: (pool_layer, 0, 0, 0)),
            pl.BlockSpec(scale_all.shape, const),
            pl.BlockSpec((1, D_MODEL, d), lambda i, j: (pool_layer, 0, 0)),
        ] + later_specs,
        out_specs=[pl.BlockSpec((1, tm, d), lambda i, j: (i, j, 0))] + later_out_specs,
        out_shape=[jax.ShapeDtypeStruct(x.shape, x.dtype)] + later_out_shape,
        scratch_shapes=[
            pltpu.VMEM((POOL_HALO + tm, D_MODEL), jnp.float32),
            pltpu.VMEM((d, 2 * D_MODEL), jnp.bfloat16),
            pltpu.VMEM((n_grp, POOL_GROUP_DIM, POOL_GROUP_DIM), jnp.bfloat16),
            pltpu.VMEM((D_MODEL, d), jnp.bfloat16),
        ],
        compiler_params=pltpu.CompilerParams(
            dimension_semantics=("arbitrary", "arbitrary"), vmem_limit_bytes=VMEM_LIMIT),
        name="pool_layer",
    )(x, mod_all, norm_g, w_in_all, w_group_all, scale_all, w_out_all, *later_args)


def kernel(x, c, positions, ada_w, ada_b, norm_g, attn_w_in, attn_q_norm, attn_k_norm, attn_sinks,
           attn_w_out, pool_w_in, pool_w_group, pool_scale, pool_w_out):
    depth = ada_w.shape[0]
    small = _attn_small_params(attn_q_norm, attn_k_norm)
    n_front = min(ADALN_UPFRONT, depth)
    mod_front = _adaln(c, ada_w, ada_b, n_front)
    mods = [(mod_front, l) for l in range(n_front)]
    for i in range(depth):
        jl = i // 2
        mod_all, mod_idx = mods[i]
        later = (c, ada_w, ada_b) if (i == n_front - 1 and depth > n_front) else None
        if i % 2 == 0:
            outs = _attn_layer(x, mod_all, mod_idx, norm_g, positions, small, attn_w_in, attn_sinks,
                               attn_w_out, i, jl, later)
        else:
            outs = _pool_layer(x, mod_all, mod_idx, norm_g, pool_w_in, pool_w_group, pool_scale,
                               pool_w_out, i, jl, later)
        x = outs[0]
        if later is not None:
            mods += [(outs[1], l) for l in range(depth - n_front)]
    return x
```

```python
import functools
import math

import jax
import jax.numpy as jnp
from jax import lax
from jax.experimental import pallas as pl
from jax.experimental.pallas import tpu as pltpu

D_MODEL = 1024
HEAD_DIM = 64
N_HEADS = 16
N_KV_HEADS = 4
GROUP = N_HEADS // N_KV_HEADS
Q_WIDTH = N_HEADS * HEAD_DIM
KV_WIDTH = N_KV_HEADS * HEAD_DIM
ATTN_IN_WIDTH = 2 * Q_WIDTH + 2 * KV_WIDTH
BLOCK = 128
SUB = BLOCK // 2
ROT_DIM = HEAD_DIM // 4
ROT_HALF = ROT_DIM // 2
ROPE_THETA = 500000.0
POOL_WINDOWS = (2, 4, 8, 16)
POOL_GROUP_DIM = D_MODEL // len(POOL_WINDOWS)
POOL_HALO = 16
NORM_EPS = 1e-6
LANES = 128
BF16_ROWS = 16
V_PAD = BF16_ROWS
V_ROWS = HEAD_DIM + V_PAD
LOG2E = math.log2(math.e)

ATTN_HALF_TILE = 256
ATTN_PARTS = 4
PROJ_CHUNK = 512
POOL_PART_TILE = 512
POOL_PARTS = 2
POOL_PROJ_CHUNK = 256
ADALN_UPFRONT = 2
VMEM_LIMIT = 60 * 1024 * 1024

_NT = (((1,), (1,)), ((), ()))
_TN = (((0,), (0,)), ((), ()))


def _silu(x):
    hx = 0.5 * x
    return hx + hx * jnp.tanh(hx)


def _adaln_kernel(c_ref, w_ref, b_ref, o_ref):
    a = _silu(c_ref[...]).astype(jnp.bfloat16)
    w = w_ref[0].astype(jnp.bfloat16)
    bias = b_ref[pl.ds(pl.program_id(0), 1), :]
    o_ref[0] = jnp.dot(a, w, preferred_element_type=jnp.float32) + bias


def _adaln(c, ada_w, ada_b, depth):
    _, d, n = ada_w.shape
    b = c.shape[0]
    tn = 1024
    return pl.pallas_call(
        _adaln_kernel,
        grid=(depth, n // tn),
        in_specs=[
            pl.BlockSpec((b, d), lambda l, j: (0, 0)),
            pl.BlockSpec((1, d, tn), lambda l, j: (l, 0, j)),
            pl.BlockSpec((ada_b.shape[0], tn), lambda l, j: (0, j)),
        ],
        out_specs=pl.BlockSpec((1, b, tn), lambda l, j: (l, 0, j)),
        out_shape=jax.ShapeDtypeStruct((depth, b, n), jnp.float32),
        compiler_params=pltpu.CompilerParams(
            dimension_semantics=("arbitrary", "arbitrary"), vmem_limit_bytes=VMEM_LIMIT),
        name="adaln_mod",
    )(c, ada_w, ada_b)


def _modulation(mod_ref, batch):
    row = mod_ref[0, pl.ds(batch, 1), :]
    return row[:, :D_MODEL], row[:, D_MODEL:2 * D_MODEL], row[:, 2 * D_MODEL:]


def _modulated_norm(x, shift, scale, g):
    ms = jnp.mean(x * x, axis=-1, keepdims=True)
    gain = g * (1.0 + scale)
    h = x * lax.rsqrt(ms + NORM_EPS) * gain + shift
    return h.astype(jnp.bfloat16)


def _later_adaln_step(later, later_mod_ref, layer):
    c_ref, w_refs, b_ref = later[0], later[1:-1], later[-1]

    @pl.when((pl.program_id(0) == 0) & (pl.program_id(1) == 0))
    def _():
        for l in range(len(w_refs)):
            later_mod_ref[l] = jnp.broadcast_to(b_ref[layer + 1 + l:layer + 2 + l, :],
                                                later_mod_ref.shape[1:])

    a = _silu(c_ref[0]).astype(jnp.bfloat16)
    for l, w_ref in enumerate(w_refs):
        later_mod_ref[l] += jnp.dot(a, w_ref[0].astype(jnp.bfloat16),
                                    preferred_element_type=jnp.float32)


def _later_adaln_operands(later_adaln, layer, b, d, steps_per_seq):
    if later_adaln is None:
        return [], [], [], [], 0
    c, ada_w, ada_b = later_adaln
    first = layer + 1
    n_later = ada_w.shape[0] - first
    n_steps = b * steps_per_seq
    rows = d // n_steps
    assert rows * n_steps == d and rows % 8 == 0
    flat = lambda i, j: i * steps_per_seq + j
    specs = ([pl.BlockSpec((1, b, rows), lambda i, j: (flat(i, j), 0, 0))]
             + [pl.BlockSpec((1, rows, 3 * d), lambda i, j, l=l: (first + l, flat(i, j), 0))
                for l in range(n_later)]
             + [pl.BlockSpec(ada_b.shape, lambda i, j: (0, 0))])
    args = [c.reshape(b, n_steps, rows).transpose(1, 0, 2)] + [ada_w] * n_later + [ada_b]
    out_specs = [pl.BlockSpec((n_later, b, 3 * d), lambda i, j: (0, 0, 0))]
    out_shape = [jax.ShapeDtypeStruct((n_later, b, 3 * d), jnp.float32)]
    return specs, args, out_specs, out_shape, n_later


def _attn_kernel(*refs, n_later, layer, attn_layer):
    n_in = 8
    x_ref, mod_ref, g_ref, pos_ref, small_ref, w_in_ref, w_out_ref, sink_ref = refs[:n_in]
    later = refs[n_in:n_in + (n_later + 2 if n_later else 0)]
    refs = refs[n_in + len(later):]
    o_ref = refs[0]
    later_mod_ref = refs[1] if n_later else None
    qa_s, qb_s, k_s, vt_s, gt_s, ogt_s, w_in_t_s, w_out_s = refs[-8:]
    tm = ATTN_HALF_TILE
    batch = pl.program_id(0)
    j = pl.program_id(1)

    shift, scale, gate = _modulation(mod_ref, batch)
    g_norm = g_ref[layer:layer + 1, :]
    lane_tile = lambda a: jnp.concatenate([a] * (tm // LANES), axis=1)
    invf = lane_tile(small_ref[0:ROT_HALF, :])
    g0 = ROT_HALF + attn_layer * 2 * HEAD_DIM
    q_gain = lane_tile(small_ref[g0:g0 + HEAD_DIM, :])
    k_gain = lane_tile(small_ref[g0 + HEAD_DIM:g0 + 2 * HEAD_DIM, :])

    if n_later:
        _later_adaln_step(later, later_mod_ref, layer)

    @pl.when((pl.program_id(0) == 0) & (j == 0))
    def _():
        for c in range(ATTN_IN_WIDTH // LANES):
            cols = slice(c * LANES, (c + 1) * LANES)
            w_in_t_s[cols, :] = w_in_ref[0, :, cols].T.astype(jnp.bfloat16)
        for c in range(Q_WIDTH // LANES):
            rows = slice(c * LANES, (c + 1) * LANES)
            w_out_s[rows, :] = w_out_ref[0, rows, :].astype(jnp.bfloat16)

    @pl.when(j == 0)
    def _():
        k_s[0:BLOCK, :] = jnp.zeros((BLOCK, k_s.shape[1]), k_s.dtype)
        vt_s[:, 0:BLOCK] = jnp.zeros((vt_s.shape[0], BLOCK), vt_s.dtype)

    kq = lax.broadcasted_iota(jnp.int32, (SUB, 2 * BLOCK), 0)
    iq = lax.broadcasted_iota(jnp.int32, (SUB, 2 * BLOCK), 1) & (SUB - 1)
    use_new = kq <= iq
    prev_bias = jnp.where(j > 0, 0.0, -jnp.inf)
    lane4 = lax.broadcasted_iota(jnp.int32, (1, 2 * BLOCK), 1) // SUB
    low_q = (lax.broadcasted_iota(jnp.int32, (HEAD_DIM, tm), 1) & SUB) == 0
    low_o = lax.broadcasted_iota(jnp.int32, (HEAD_DIM, LANES), 1) < SUB
    zero = jnp.zeros((HEAD_DIM, LANES), jnp.bfloat16)
    zero_p = jnp.zeros((SUB, 2 * BLOCK), jnp.bfloat16)
    ones = jnp.ones((V_PAD, tm), jnp.bfloat16)

    def projection(hf):
        toks = slice(hf * tm, (hf + 1) * tm)
        krows = slice(BLOCK + hf * tm, BLOCK + (hf + 1) * tm)
        st = {}

        def norm():
            st["h"] = _modulated_norm(x_ref[0, toks, :], shift, scale, g_norm)
            ang = pos_ref[pl.ds(batch, 1), toks].astype(jnp.float32) * invf
            st["cos"], st["sin"] = jnp.cos(ang), jnp.sin(ang)

        def norm_rope(t, gain):
            ssq = jnp.sum(t * t, axis=0, keepdims=True)
            tn = t * lax.rsqrt(ssq * (1.0 / HEAD_DIM) + NORM_EPS) * gain
            x1, x2 = tn[0:ROT_HALF], tn[ROT_HALF:ROT_DIM]
            cos, sin = st["cos"], st["sin"]
            return jnp.concatenate([x1 * cos - x2 * sin, x2 * cos + x1 * sin, tn[ROT_DIM:]], axis=0)

        def chunk(c):
            row0 = c * PROJ_CHUNK
            pt = lax.dot_general(w_in_t_s[row0:row0 + PROJ_CHUNK, :], st["h"], _NT,
                                 preferred_element_type=jnp.float32)
            if row0 >= Q_WIDTH + 2 * KV_WIDTH:
                g0 = row0 - (Q_WIDTH + 2 * KV_WIDTH)
                gt_s[g0:g0 + PROJ_CHUNK, toks] = _silu(pt)
                return
            for i in range(PROJ_CHUNK // HEAD_DIM):
                t = pt[i * HEAD_DIM:(i + 1) * HEAD_DIM]
                hd = row0 // HEAD_DIM + i
                if hd < N_HEADS:
                    qn = norm_rope(t, q_gain)
                    keep = low_q if hd % 2 == 0 else ~low_q
                    swapped = jnp.concatenate(
                        [pltpu.roll(qn[:, c2 * LANES:(c2 + 1) * LANES], SUB, 1)
                         for c2 in range(tm // LANES)], axis=1)
                    hrows = slice(hd * HEAD_DIM, (hd + 1) * HEAD_DIM)
                    qa_s[hrows, toks] = jnp.where(keep, qn, 0.0).astype(jnp.bfloat16)
                    qb_s[hrows, toks] = jnp.where(keep, swapped, 0.0).astype(jnp.bfloat16)
                elif hd < N_HEADS + N_KV_HEADS:
                    kh = hd - N_HEADS
                    kt = norm_rope(t, k_gain)
                    k_s[krows, kh * LANES:(kh + 1) * LANES] = (
                        jnp.concatenate([kt, kt], axis=0).T.astype(jnp.bfloat16))
                else:
                    kh = hd - N_HEADS - N_KV_HEADS
                    vt_s[kh * V_ROWS:kh * V_ROWS + HEAD_DIM, krows] = t.astype(jnp.bfloat16)
                    vt_s[kh * V_ROWS + HEAD_DIM:(kh + 1) * V_ROWS, krows] = ones

        def rank(c):
            row0 = c * PROJ_CHUNK
            return 1 if row0 < Q_WIDTH else (0 if row0 < Q_WIDTH + 2 * KV_WIDTH else 2)

        order = sorted(range(ATTN_IN_WIDTH // PROJ_CHUNK), key=rank)
        return [norm] + [(lambda c=c: chunk(c)) for c in order]

    def scores(qb):
        qcols = slice(qb * BLOCK, (qb + 1) * BLOCK)
        out = []
        for kh in range(N_KV_HEADS):
            tiles = []
            for hq in range(2):
                u = 2 * qb + hq
                kd = k_s[u * SUB:(u + 3) * SUB, kh * LANES:(kh + 1) * LANES]
                blocks = []
                for g in range(GROUP):
                    hrows = slice((kh * GROUP + g) * HEAD_DIM, (kh * GROUP + g + 1) * HEAD_DIM)
                    src = (qa_s if g % 2 == hq else qb_s)[hrows, qcols]
                    blocks.append(jnp.concatenate([src, zero] if g < 2 else [zero, src], axis=1))
                sc = jnp.dot(jnp.concatenate([kd, kd], axis=1), jnp.concatenate(blocks, axis=0),
                             preferred_element_type=jnp.float32)
                old, mid, new = sc[:SUB], sc[SUB:2 * SUB], sc[2 * SUB:]
                if qb == 0:
                    old = old + prev_bias
                    if hq == 0:
                        mid = mid + prev_bias
                tiles.append(jnp.concatenate([mid, jnp.where(use_new, new, old)], axis=0))
            out.append(tiles)
        return out

    def finish(qb, tiles_all):
        qcols = slice(qb * BLOCK, (qb + 1) * BLOCK)
        keys = slice(qb * BLOCK, (qb + 2) * BLOCK)
        for kh, tiles in enumerate(tiles_all):
            vt = vt_s[kh * V_ROWS:(kh + 1) * V_ROWS, keys]
            h0 = kh * GROUP
            sink = jnp.where(lane4 == 0, sink_ref[attn_layer, h0], jnp.where(
                lane4 == 1, sink_ref[attn_layer, h0 + 1], jnp.where(
                    lane4 == 2, sink_ref[attn_layer, h0 + 2],
                    sink_ref[attn_layer, h0 + 3]))) * LOG2E
            outs = []
            for hq, sc in enumerate(tiles):
                m = jnp.maximum(jnp.max(sc, axis=0, keepdims=True), sink)
                p = jnp.exp2(sc - m).astype(jnp.bfloat16)
                p_mid, p_fold = p[:SUB], p[SUB:]
                window = [jnp.where(use_new, zero_p, p_fold), p_mid, jnp.where(use_new, p_fold, zero_p)]
                p2 = jnp.concatenate(window + [zero_p] if hq == 0 else [zero_p] + window, axis=0)
                o = jnp.dot(vt, p2, preferred_element_type=jnp.float32)
                denom = o[HEAD_DIM:HEAD_DIM + 1] + jnp.exp2(sink - m)
                outs.append(o[:HEAD_DIM] * (1.0 / denom))
            for g in range(GROUP):
                par = g % 2
                cv = slice((g // 2) * LANES, (g // 2 + 1) * LANES)
                moved = pltpu.roll(outs[1 - par][:, cv], SUB, 1)
                full = jnp.where(low_o if par == 0 else ~low_o, outs[par][:, cv], moved)
                hrows = slice((h0 + g) * HEAD_DIM, (h0 + g + 1) * HEAD_DIM)
                ogt_s[hrows, qcols] = (full * gt_s[hrows, qcols]).astype(jnp.bfloat16)

    def attention(hf):
        blocks = range(hf * (tm // BLOCK), (hf + 1) * (tm // BLOCK))
        st = {}

        def score_step(qb):
            st[qb] = scores(qb)

        def finish_step(qb):
            finish(qb, st.pop(qb))

        steps = [lambda: score_step(blocks[0])]
        for qb in blocks:
            if qb + 1 in blocks:
                steps.append(lambda qb=qb: score_step(qb + 1))
            steps.append(lambda qb=qb: finish_step(qb))
        return steps

    def out_projection(hf):
        toks = slice(hf * tm, (hf + 1) * tm)

        def run():
            out = lax.dot_general(ogt_s[:, toks], w_out_s[...], _TN,
                                  preferred_element_type=jnp.float32)
            o_ref[0, toks, :] = x_ref[0, toks, :] + gate * out

        return [run]

    def interleave(a, b):
        done_a = done_b = 0
        while done_a < len(a) or done_b < len(b):
            if done_b >= len(b) or (done_a < len(a) and done_a * len(b) <= done_b * len(a)):
                a[done_a]()
                done_a += 1
            else:
                b[done_b]()
                done_b += 1

    for f in projection(0):
        f()
    for part in range(1, ATTN_PARTS):
        proj = projection(part)
        proj[0]()
        interleave(attention(part - 1) + out_projection(part - 1), proj[1:])
    for f in attention(ATTN_PARTS - 1) + out_projection(ATTN_PARTS - 1):
        f()

    step_tokens = ATTN_PARTS * tm
    k_s[0:BLOCK, :] = k_s[step_tokens:step_tokens + BLOCK, :]
    vt_s[:, 0:BLOCK] = vt_s[:, step_tokens:step_tokens + BLOCK]


def _attn_small_params(q_norm, k_norm):
    inv_freq = ROPE_THETA ** (-jnp.arange(ROT_HALF, dtype=jnp.float32) * 2.0 / ROT_DIM)
    cols = [inv_freq]
    for l in range(q_norm.shape[0]):
        cols += [q_norm[l] * (HEAD_DIM ** -0.5 * LOG2E), k_norm[l]]
    col = jnp.concatenate(cols)
    return jnp.broadcast_to(col[:, None], (col.shape[0], LANES))


def _attn_layer(x, mod_all, mod_idx, norm_g, positions, small, w_in_all, sinks, w_out_all,
                layer, attn_layer, later_adaln=None):
    b, s, d = x.shape
    tm = ATTN_HALF_TILE
    step = ATTN_PARTS * tm
    steps_per_seq = s // step
    const = lambda i, j: (0, 0)
    later_specs, later_args, later_out_specs, later_out_shape, n_later = _later_adaln_operands(
        later_adaln, layer, b, d, steps_per_seq)
    out_specs = [pl.BlockSpec((1, step, d), lambda i, j: (i, j, 0))] + later_out_specs
    out_shape = [jax.ShapeDtypeStruct(x.shape, x.dtype)] + later_out_shape
    return pl.pallas_call(
        functools.partial(_attn_kernel, n_later=n_later, layer=layer, attn_layer=attn_layer),
        grid=(b, steps_per_seq),
        in_specs=[
            pl.BlockSpec((1, step, d), lambda i, j: (i, j, 0)),
            pl.BlockSpec((1, b, 3 * d), lambda i, j: (mod_idx, 0, 0)),
            pl.BlockSpec(norm_g.shape, const),
            pl.BlockSpec((b, step), lambda i, j: (0, j)),
            pl.BlockSpec(small.shape, const),
            pl.BlockSpec((1, d, ATTN_IN_WIDTH), lambda i, j: (attn_layer, 0, 0)),
            pl.BlockSpec((1, Q_WIDTH, d), lambda i, j: (attn_layer, 0, 0)),
            pl.BlockSpec(memory_space=pltpu.SMEM),
        ] + later_specs,
        out_specs=out_specs,
        out_shape=out_shape,
        scratch_shapes=[
            pltpu.VMEM((Q_WIDTH, step), jnp.bfloat16),
            pltpu.VMEM((Q_WIDTH, step), jnp.bfloat16),
            pltpu.VMEM((BLOCK + step, N_KV_HEADS * LANES), jnp.bfloat16),
            pltpu.VMEM((N_KV_HEADS * V_ROWS, BLOCK + step), jnp.bfloat16),
            pltpu.VMEM((Q_WIDTH, step), jnp.float32),
            pltpu.VMEM((Q_WIDTH, step), jnp.bfloat16),
            pltpu.VMEM((ATTN_IN_WIDTH, d), jnp.bfloat16),
            pltpu.VMEM((Q_WIDTH, d), jnp.bfloat16),
        ],
        compiler_params=pltpu.CompilerParams(
            dimension_semantics=("arbitrary", "arbitrary"), vmem_limit_bytes=VMEM_LIMIT),
        name="attn_layer",
    )(x, mod_all, norm_g, positions, small, w_in_all, w_out_all, sinks, *later_args)


def _pool_kernel(*refs, n_later, layer, pool_layer):
    n_in = 7
    x_ref, mod_ref, g_ref, w_in_ref, w_grp_ref, scale_ref, w_out_ref = refs[:n_in]
    later = refs[n_in:n_in + (n_later + 2 if n_later else 0)]
    o_ref = refs[n_in + len(later)]
    v_s, w_in_s, w_grp_s, w_out_s = refs[-4:]
    tm = POOL_PART_TILE
    j = pl.program_id(1)
    if n_later:
        _later_adaln_step(later, refs[n_in + len(later) + 1], layer)
    shift, scale, gate = _modulation(mod_ref, pl.program_id(0))
    g_norm = g_ref[layer:layer + 1, :]
    mix_scale = scale_ref[pool_layer:pool_layer + 1, :]

    @pl.when((pl.program_id(0) == 0) & (j == 0))
    def _():
        for c in range(D_MODEL // LANES):
            rows = slice(c * LANES, (c + 1) * LANES)
            w_in_s[rows, :] = w_in_ref[0, rows, :].astype(jnp.bfloat16)
            w_out_s[rows, :] = w_out_ref[0, rows, :].astype(jnp.bfloat16)
        for gi in range(len(POOL_WINDOWS)):
            w_grp_s[gi] = w_grp_ref[0, gi].astype(jnp.bfloat16)

    @pl.when(j == 0)
    def _():
        v_s[0:POOL_HALO, :] = jnp.zeros((POOL_HALO, v_s.shape[1]), v_s.dtype)

    gates = {}

    def projection(hf):
        toks = slice(hf * tm, (hf + 1) * tm)
        st = {}

        def norm():
            st["h"] = _modulated_norm(x_ref[0, toks, :], shift, scale, g_norm)

        def value_path(cq):
            cols = slice(cq * POOL_PROJ_CHUNK, (cq + 1) * POOL_PROJ_CHUNK)
            v_s[POOL_HALO + hf * tm:POOL_HALO + (hf + 1) * tm, cols] = jnp.dot(
                st["h"], w_in_s[:, cols], preferred_element_type=jnp.float32)

        def gate_path(cq):
            cols = slice(D_MODEL + cq * POOL_PROJ_CHUNK, D_MODEL + (cq + 1) * POOL_PROJ_CHUNK)
            gates[hf, cq] = _silu(jnp.dot(st["h"], w_in_s[:, cols],
                                          preferred_element_type=jnp.float32))

        n_chunks = D_MODEL // POOL_PROJ_CHUNK
        return ([norm] + [(lambda cq=cq: value_path(cq)) for cq in range(n_chunks)]
                + [(lambda cq=cq: gate_path(cq)) for cq in range(n_chunks)])

    def mixing(hf):
        toks = slice(hf * tm, (hf + 1) * tm)
        ext = slice(hf * tm, POOL_HALO + (hf + 1) * tm)
        t = lax.broadcasted_iota(jnp.int32, (tm, POOL_GROUP_DIM), 0) + (j * POOL_PARTS + hf) * tm
        mixed = []

        def group(gi):
            w = POOL_WINDOWS[gi]
            cols = slice(gi * POOL_GROUP_DIM, (gi + 1) * POOL_GROUP_DIM)
            win = v_s[ext, cols]
            cur = win[POOL_HALO:]
            span = 1
            while span < w:
                win = win + pltpu.roll(win, span, 0)
                span *= 2
            count = jnp.minimum(t + 1, w).astype(jnp.float32)
            pooled = win[POOL_HALO:] / count - cur
            mixed.append(jnp.dot(pooled.astype(jnp.bfloat16), w_grp_s[gi],
                                 preferred_element_type=jnp.float32))

        def out():
            m = jnp.concatenate(mixed, axis=1) * mix_scale
            g_all = jnp.concatenate(
                [gates[hf, cq] for cq in range(D_MODEL // POOL_PROJ_CHUNK)], axis=1)
            o = jnp.dot((m * g_all).astype(jnp.bfloat16), w_out_s[...],
                        preferred_element_type=jnp.float32)
            o_ref[0, toks, :] = x_ref[0, toks, :] + gate * o

        return [(lambda gi=gi: group(gi)) for gi in range(len(POOL_WINDOWS))] + [out]

    def interleave(a, b):
        done_a = done_b = 0
        while done_a < len(a) or done_b < len(b):
            if done_b >= len(b) or (done_a < len(a) and done_a * len(b) <= done_b * len(a)):
                a[done_a]()
                done_a += 1
            else:
                b[done_b]()
                done_b += 1

    for f in projection(0):
        f()
    for part in range(1, POOL_PARTS):
        proj = projection(part)
        proj[0]()
        interleave(mixing(part - 1), proj[1:])
    for f in mixing(POOL_PARTS - 1):
        f()

    v_s[0:POOL_HALO, :] = v_s[POOL_PARTS * tm:POOL_PARTS * tm + POOL_HALO, :]


def _pool_layer(x, mod_all, mod_idx, norm_g, w_in_all, w_group_all, scale_all, w_out_all,
                layer, pool_layer, later_adaln=None):
    b, s, d = x.shape
    tm = POOL_PARTS * POOL_PART_TILE
    n_grp = len(POOL_WINDOWS)
    const = lambda i, j: (0, 0)
    later_specs, later_args, later_out_specs, later_out_shape, n_later = _later_adaln_operands(
        later_adaln, layer, b, d, s // tm)
    return pl.pallas_call(
        functools.partial(_pool_kernel, n_later=n_later, layer=layer, pool_layer=pool_layer),
        grid=(b, s // tm),
        in_specs=[
            pl.BlockSpec((1, tm, d), lambda i, j: (i, j, 0)),
            pl.BlockSpec((1, b, 3 * d), lambda i, j: (mod_idx, 0, 0)),
            pl.BlockSpec(norm_g.shape, const),
            pl.BlockSpec((1, d, 2 * D_MODEL), lambda i, j: (pool_layer, 0, 0)),
            pl.BlockSpec((1, n_grp, POOL_GROUP_DIM, POOL_GROUP_DIM), lambda i, j: (pool_layer, 0, 0, 0)),
            pl.BlockSpec(scale_all.shape, const),
            pl.BlockSpec((1, D_MODEL, d), lambda i, j: (pool_layer, 0, 0)),
        ] + later_specs,
        out_specs=[pl.BlockSpec((1, tm, d), lambda i, j: (i, j, 0))] + later_out_specs,
        out_shape=[jax.ShapeDtypeStruct(x.shape, x.dtype)] + later_out_shape,
        scratch_shapes=[
            pltpu.VMEM((POOL_HALO + tm, D_MODEL), jnp.float32),
            pltpu.VMEM((d, 2 * D_MODEL), jnp.bfloat16),
            pltpu.VMEM((n_grp, POOL_GROUP_DIM, POOL_GROUP_DIM), jnp.bfloat16),
            pltpu.VMEM((D_MODEL, d), jnp.bfloat16),
        ],
        compiler_params=pltpu.CompilerParams(
            dimension_semantics=("arbitrary", "arbitrary"), vmem_limit_bytes=VMEM_LIMIT),
        name="pool_layer",
    )(x, mod_all, norm_g, w_in_all, w_group_all, scale_all, w_out_all, *later_args)


def kernel(x, c, positions, ada_w, ada_b, norm_g, attn_w_in, attn_q_norm, attn_k_norm, attn_sinks,
           attn_w_out, pool_w_in, pool_w_group, pool_scale, pool_w_out):
    depth = ada_w.shape[0]
    small = _attn_small_params(attn_q_norm, attn_k_norm)
    n_front = min(ADALN_UPFRONT, depth)
    mod_front = _adaln(c, ada_w, ada_b, n_front)
    mods = [(mod_front, l) for l in range(n_front)]
    for i in range(depth):
        jl = i // 2
        mod_all, mod_idx = mods[i]
        later = (c, ada_w, ada_b) if (i == n_front - 1 and depth > n_front) else None
        if i % 2 == 0:
            outs = _attn_layer(x, mod_all, mod_idx, norm_g, positions, small, attn_w_in, attn_sinks,
                               attn_w_out, i, jl, later)
        else:
            outs = _pool_layer(x, mod_all, mod_idx, norm_g, pool_w_in, pool_w_group, pool_scale,
                               pool_w_out, i, jl, later)
        x = outs[0]
        if later is not None:
            mods += [(outs[1], l) for l in range(depth - n_front)]
    return x
```

```python
import functools
import math

import jax
import jax.numpy as jnp
from jax import lax
from jax.experimental import pallas as pl
from jax.experimental.pallas import tpu as pltpu

D_MODEL = 1024
HEAD_DIM = 64
N_HEADS = 16
N_KV_HEADS = 4
GROUP = N_HEADS // N_KV_HEADS
Q_WIDTH = N_HEADS * HEAD_DIM
KV_WIDTH = N_KV_HEADS * HEAD_DIM
ATTN_IN_WIDTH = 2 * Q_WIDTH + 2 * KV_WIDTH
BLOCK = 128
SUB = BLOCK // 2
ROT_DIM = HEAD_DIM // 4
ROT_HALF = ROT_DIM // 2
ROPE_THETA = 500000.0
POOL_WINDOWS = (2, 4, 8, 16)
POOL_GROUP_DIM = D_MODEL // len(POOL_WINDOWS)
POOL_HALO = 16
NORM_EPS = 1e-6
LANES = 128
BF16_ROWS = 16
V_PAD = BF16_ROWS
V_ROWS = HEAD_DIM + V_PAD
LOG2E = math.log2(math.e)

ATTN_HALF_TILE = 256
ATTN_PARTS = 8
PROJ_CHUNK = 512
POOL_PART_TILE = 512
POOL_PARTS = 2
POOL_PROJ_CHUNK = 256
ADALN_UPFRONT = 2
VMEM_LIMIT = 64 * 1024 * 1024

_NT = (((1,), (1,)), ((), ()))
_TN = (((0,), (0,)), ((), ()))


def _silu(x):
    hx = 0.5 * x
    return hx + hx * jnp.tanh(hx)


def _adaln_kernel(c_ref, w_ref, b_ref, o_ref):
    a = _silu(c_ref[...]).astype(jnp.bfloat16)
    w = w_ref[0].astype(jnp.bfloat16)
    bias = b_ref[pl.ds(pl.program_id(0), 1), :]
    o_ref[0] = jnp.dot(a, w, preferred_element_type=jnp.float32) + bias


def _adaln(c, ada_w, ada_b, depth):
    _, d, n = ada_w.shape
    b = c.shape[0]
    tn = 1024
    return pl.pallas_call(
        _adaln_kernel,
        grid=(depth, n // tn),
        in_specs=[
            pl.BlockSpec((b, d), lambda l, j: (0, 0)),
            pl.BlockSpec((1, d, tn), lambda l, j: (l, 0, j)),
            pl.BlockSpec((ada_b.shape[0], tn), lambda l, j: (0, j)),
        ],
        out_specs=pl.BlockSpec((1, b, tn), lambda l, j: (l, 0, j)),
        out_shape=jax.ShapeDtypeStruct((depth, b, n), jnp.float32),
        compiler_params=pltpu.CompilerParams(
            dimension_semantics=("arbitrary", "arbitrary"), vmem_limit_bytes=VMEM_LIMIT),
        name="adaln_mod",
    )(c, ada_w, ada_b)


def _modulation(mod_ref, batch):
    row = mod_ref[0, pl.ds(batch, 1), :]
    return row[:, :D_MODEL], row[:, D_MODEL:2 * D_MODEL], row[:, 2 * D_MODEL:]


def _modulated_norm(x, shift, scale, g):
    ms = jnp.mean(x * x, axis=-1, keepdims=True)
    gain = g * (1.0 + scale)
    h = x * lax.rsqrt(ms + NORM_EPS) * gain + shift
    return h.astype(jnp.bfloat16)


def _later_adaln_step(later, later_mod_ref, layer):
    c_ref, w_refs, b_ref = later[0], later[1:-1], later[-1]

    @pl.when((pl.program_id(0) == 0) & (pl.program_id(1) == 0))
    def _():
        for l in range(len(w_refs)):
            later_mod_ref[l] = jnp.broadcast_to(b_ref[layer + 1 + l:layer + 2 + l, :],
                                                later_mod_ref.shape[1:])

    a = _silu(c_ref[0]).astype(jnp.bfloat16)
    for l, w_ref in enumerate(w_refs):
        later_mod_ref[l] += jnp.dot(a, w_ref[0].astype(jnp.bfloat16),
                                    preferred_element_type=jnp.float32)


def _later_adaln_operands(later_adaln, layer, b, d, steps_per_seq):
    if later_adaln is None:
        return [], [], [], [], 0
    c, ada_w, ada_b = later_adaln
    first = layer + 1
    n_later = ada_w.shape[0] - first
    n_steps = b * steps_per_seq
    rows = d // n_steps
    assert rows * n_steps == d and rows % 8 == 0
    flat = lambda i, j: i * steps_per_seq + j
    specs = ([pl.BlockSpec((1, b, rows), lambda i, j: (flat(i, j), 0, 0))]
             + [pl.BlockSpec((1, rows, 3 * d), lambda i, j, l=l: (first + l, flat(i, j), 0))
                for l in range(n_later)]
             + [pl.BlockSpec(ada_b.shape, lambda i, j: (0, 0))])
    args = [c.reshape(b, n_steps, rows).transpose(1, 0, 2)] + [ada_w] * n_later + [ada_b]
    out_specs = [pl.BlockSpec((n_later, b, 3 * d), lambda i, j: (0, 0, 0))]
    out_shape = [jax.ShapeDtypeStruct((n_later, b, 3 * d), jnp.float32)]
    return specs, args, out_specs, out_shape, n_later


def _attn_kernel(*refs, n_later, layer, attn_layer):
    n_in = 8
    x_ref, mod_ref, g_ref, pos_ref, small_ref, w_in_ref, w_out_ref, sink_ref = refs[:n_in]
    later = refs[n_in:n_in + (n_later + 2 if n_later else 0)]
    refs = refs[n_in + len(later):]
    o_ref = refs[0]
    later_mod_ref = refs[1] if n_later else None
    qa_s, qb_s, k_s, vt_s, gt_s, ogt_s, w_in_t_s, w_out_s = refs[-8:]
    tm = ATTN_HALF_TILE
    batch = pl.program_id(0)
    j = pl.program_id(1)

    shift, scale, gate = _modulation(mod_ref, batch)
    g_norm = g_ref[layer:layer + 1, :]
    lane_tile = lambda a: jnp.concatenate([a] * (tm // LANES), axis=1)
    invf = lane_tile(small_ref[0:ROT_HALF, :])
    g0 = ROT_HALF + attn_layer * 2 * HEAD_DIM
    q_gain = lane_tile(small_ref[g0:g0 + HEAD_DIM, :])
    k_gain = lane_tile(small_ref[g0 + HEAD_DIM:g0 + 2 * HEAD_DIM, :])

    if n_later:
        _later_adaln_step(later, later_mod_ref, layer)

    @pl.when((pl.program_id(0) == 0) & (j == 0))
    def _():
        for c in range(ATTN_IN_WIDTH // LANES):
            cols = slice(c * LANES, (c + 1) * LANES)
            w_in_t_s[cols, :] = w_in_ref[0, :, cols].T.astype(jnp.bfloat16)
        for c in range(Q_WIDTH // LANES):
            rows = slice(c * LANES, (c + 1) * LANES)
            w_out_s[rows, :] = w_out_ref[0, rows, :].astype(jnp.bfloat16)

    @pl.when(j == 0)
    def _():
        k_s[0:BLOCK, :] = jnp.zeros((BLOCK, k_s.shape[1]), k_s.dtype)
        vt_s[:, 0:BLOCK] = jnp.zeros((vt_s.shape[0], BLOCK), vt_s.dtype)

    kq = lax.broadcasted_iota(jnp.int32, (SUB, 2 * BLOCK), 0)
    iq = lax.broadcasted_iota(jnp.int32, (SUB, 2 * BLOCK), 1) & (SUB - 1)
    use_new = kq <= iq
    prev_bias = jnp.where(j > 0, 0.0, -jnp.inf)
    lane4 = lax.broadcasted_iota(jnp.int32, (1, 2 * BLOCK), 1) // SUB
    low_q = (lax.broadcasted_iota(jnp.int32, (HEAD_DIM, tm), 1) & SUB) == 0
    low_o = lax.broadcasted_iota(jnp.int32, (HEAD_DIM, LANES), 1) < SUB
    zero = jnp.zeros((HEAD_DIM, LANES), jnp.bfloat16)
    zero_p = jnp.zeros((SUB, 2 * BLOCK), jnp.bfloat16)
    ones = jnp.ones((V_PAD, tm), jnp.bfloat16)

    def projection(hf):
        toks = slice(hf * tm, (hf + 1) * tm)
        ring = slice((hf % 2) * tm, (hf % 2 + 1) * tm)
        krows = slice(BLOCK + hf * tm, BLOCK + (hf + 1) * tm)
        st = {}

        def norm():
            st["h"] = _modulated_norm(x_ref[0, toks, :], shift, scale, g_norm)
            ang = pos_ref[pl.ds(batch, 1), toks].astype(jnp.float32) * invf
            st["cos"], st["sin"] = jnp.cos(ang), jnp.sin(ang)

        def norm_rope(t, gain):
            ssq = jnp.sum(t * t, axis=0, keepdims=True)
            tn = t * lax.rsqrt(ssq * (1.0 / HEAD_DIM) + NORM_EPS) * gain
            x1, x2 = tn[0:ROT_HALF], tn[ROT_HALF:ROT_DIM]
            cos, sin = st["cos"], st["sin"]
            return jnp.concatenate([x1 * cos - x2 * sin, x2 * cos + x1 * sin, tn[ROT_DIM:]], axis=0)

        def chunk(c):
            row0 = c * PROJ_CHUNK
            pt = lax.dot_general(w_in_t_s[row0:row0 + PROJ_CHUNK, :], st["h"], _NT,
                                 preferred_element_type=jnp.float32)
            if row0 >= Q_WIDTH + 2 * KV_WIDTH:
                g0 = row0 - (Q_WIDTH + 2 * KV_WIDTH)
                gt_s[g0:g0 + PROJ_CHUNK, ring] = _silu(pt)
                return
            for i in range(PROJ_CHUNK // HEAD_DIM):
                t = pt[i * HEAD_DIM:(i + 1) * HEAD_DIM]
                hd = row0 // HEAD_DIM + i
                if hd < N_HEADS:
                    qn = norm_rope(t, q_gain)
                    keep = low_q if hd % 2 == 0 else ~low_q
                    swapped = jnp.concatenate(
                        [pltpu.roll(qn[:, c2 * LANES:(c2 + 1) * LANES], SUB, 1)
                         for c2 in range(tm // LANES)], axis=1)
                    hrows = slice(hd * HEAD_DIM, (hd + 1) * HEAD_DIM)
                    qa_s[hrows, ring] = jnp.where(keep, qn, 0.0).astype(jnp.bfloat16)
                    qb_s[hrows, ring] = jnp.where(keep, swapped, 0.0).astype(jnp.bfloat16)
                elif hd < N_HEADS + N_KV_HEADS:
                    kh = hd - N_HEADS
                    kt = norm_rope(t, k_gain)
                    k_s[krows, kh * LANES:(kh + 1) * LANES] = (
                        jnp.concatenate([kt, kt], axis=0).T.astype(jnp.bfloat16))
                else:
                    kh = hd - N_HEADS - N_KV_HEADS
                    vt_s[kh * V_ROWS:kh * V_ROWS + HEAD_DIM, krows] = t.astype(jnp.bfloat16)
                    vt_s[kh * V_ROWS + HEAD_DIM:(kh + 1) * V_ROWS, krows] = ones

        def rank(c):
            row0 = c * PROJ_CHUNK
            return 1 if row0 < Q_WIDTH else (0 if row0 < Q_WIDTH + 2 * KV_WIDTH else 2)

        order = sorted(range(ATTN_IN_WIDTH // PROJ_CHUNK), key=rank)
        return [norm] + [(lambda c=c: chunk(c)) for c in order]

    def ring_cols(qb):
        rq = qb % (2 * tm // BLOCK)
        return slice(rq * BLOCK, (rq + 1) * BLOCK)

    def scores(qb):
        qcols = ring_cols(qb)
        out = []
        for kh in range(N_KV_HEADS):
            tiles = []
            for hq in range(2):
                u = 2 * qb + hq
                kd = k_s[u * SUB:(u + 3) * SUB, kh * LANES:(kh + 1) * LANES]
                blocks = []
                for g in range(GROUP):
                    hrows = slice((kh * GROUP + g) * HEAD_DIM, (kh * GROUP + g + 1) * HEAD_DIM)
                    src = (qa_s if g % 2 == hq else qb_s)[hrows, qcols]
                    blocks.append(jnp.concatenate([src, zero] if g < 2 else [zero, src], axis=1))
                sc = jnp.dot(jnp.concatenate([kd, kd], axis=1), jnp.concatenate(blocks, axis=0),
                             preferred_element_type=jnp.float32)
                old, mid, new = sc[:SUB], sc[SUB:2 * SUB], sc[2 * SUB:]
                if qb == 0:
                    old = old + prev_bias
                    if hq == 0:
                        mid = mid + prev_bias
                tiles.append(jnp.concatenate([mid, jnp.where(use_new, new, old)], axis=0))
            out.append(tiles)
        return out

    def finish(qb, tiles_all):
        qcols = ring_cols(qb)
        keys = slice(qb * BLOCK, (qb + 2) * BLOCK)
        for kh, tiles in enumerate(tiles_all):
            vt = vt_s[kh * V_ROWS:(kh + 1) * V_ROWS, keys]
            h0 = kh * GROUP
            sink = jnp.where(lane4 == 0, sink_ref[attn_layer, h0], jnp.where(
                lane4 == 1, sink_ref[attn_layer, h0 + 1], jnp.where(
                    lane4 == 2, sink_ref[attn_layer, h0 + 2],
                    sink_ref[attn_layer, h0 + 3]))) * LOG2E
            outs = []
            for hq, sc in enumerate(tiles):
                m = jnp.maximum(jnp.max(sc, axis=0, keepdims=True), sink)
                p = jnp.exp2(sc - m).astype(jnp.bfloat16)
                p_mid, p_fold = p[:SUB], p[SUB:]
                window = [jnp.where(use_new, zero_p, p_fold), p_mid, jnp.where(use_new, p_fold, zero_p)]
                p2 = jnp.concatenate(window + [zero_p] if hq == 0 else [zero_p] + window, axis=0)
                o = jnp.dot(vt, p2, preferred_element_type=jnp.float32)
                denom = o[HEAD_DIM:HEAD_DIM + 1] + jnp.exp2(sink - m)
                outs.append(o[:HEAD_DIM] * (1.0 / denom))
            for g in range(GROUP):
                par = g % 2
                cv = slice((g // 2) * LANES, (g // 2 + 1) * LANES)
                moved = pltpu.roll(outs[1 - par][:, cv], SUB, 1)
                full = jnp.where(low_o if par == 0 else ~low_o, outs[par][:, cv], moved)
                hrows = slice((h0 + g) * HEAD_DIM, (h0 + g + 1) * HEAD_DIM)
                ogt_s[hrows, qcols] = (full * gt_s[hrows, qcols]).astype(jnp.bfloat16)

    def attention(hf):
        blocks = range(hf * (tm // BLOCK), (hf + 1) * (tm // BLOCK))
        st = {}

        def score_step(qb):
            st[qb] = scores(qb)

        def finish_step(qb):
            finish(qb, st.pop(qb))

        steps = [lambda: score_step(blocks[0])]
        for qb in blocks:
            if qb + 1 in blocks:
                steps.append(lambda qb=qb: score_step(qb + 1))
            steps.append(lambda qb=qb: finish_step(qb))
        return steps

    def out_projection(hf):
        toks = slice(hf * tm, (hf + 1) * tm)
        ring = slice((hf % 2) * tm, (hf % 2 + 1) * tm)

        def run():
            out = lax.dot_general(ogt_s[:, ring], w_out_s[...], _TN,
                                  preferred_element_type=jnp.float32)
            o_ref[0, toks, :] = x_ref[0, toks, :] + gate * out

        return [run]

    def interleave(a, b):
        done_a = done_b = 0
        while done_a < len(a) or done_b < len(b):
            if done_b >= len(b) or (done_a < len(a) and done_a * len(b) <= done_b * len(a)):
                a[done_a]()
                done_a += 1
            else:
                b[done_b]()
                done_b += 1

    for f in projection(0):
        f()
    for part in range(1, ATTN_PARTS):
        proj = projection(part)
        proj[0]()
        interleave(attention(part - 1) + out_projection(part - 1), proj[1:])
    for f in attention(ATTN_PARTS - 1) + out_projection(ATTN_PARTS - 1):
        f()

    step_tokens = ATTN_PARTS * tm
    k_s[0:BLOCK, :] = k_s[step_tokens:step_tokens + BLOCK, :]
    vt_s[:, 0:BLOCK] = vt_s[:, step_tokens:step_tokens + BLOCK]


def _attn_small_params(q_norm, k_norm):
    inv_freq = ROPE_THETA ** (-jnp.arange(ROT_HALF, dtype=jnp.float32) * 2.0 / ROT_DIM)
    cols = [inv_freq]
    for l in range(q_norm.shape[0]):
        cols += [q_norm[l] * (HEAD_DIM ** -0.5 * LOG2E), k_norm[l]]
    col = jnp.concatenate(cols)
    return jnp.broadcast_to(col[:, None], (col.shape[0], LANES))


def _attn_layer(x, mod_all, mod_idx, norm_g, positions, small, w_in_all, sinks, w_out_all,
                layer, attn_layer, later_adaln=None):
    b, s, d = x.shape
    tm = ATTN_HALF_TILE
    step = ATTN_PARTS * tm
    steps_per_seq = s // step
    const = lambda i, j: (0, 0)
    later_specs, later_args, later_out_specs, later_out_shape, n_later = _later_adaln_operands(
        later_adaln, layer, b, d, steps_per_seq)
    out_specs = [pl.BlockSpec((1, step, d), lambda i, j: (i, j, 0))] + later_out_specs
    out_shape = [jax.ShapeDtypeStruct(x.shape, x.dtype)] + later_out_shape
    return pl.pallas_call(
        functools.partial(_attn_kernel, n_later=n_later, layer=layer, attn_layer=attn_layer),
        grid=(b, steps_per_seq),
        in_specs=[
            pl.BlockSpec((1, step, d), lambda i, j: (i, j, 0)),
            pl.BlockSpec((1, b, 3 * d), lambda i, j: (mod_idx, 0, 0)),
            pl.BlockSpec(norm_g.shape, const),
            pl.BlockSpec((b, step), lambda i, j: (0, j)),
            pl.BlockSpec(small.shape, const),
            pl.BlockSpec((1, d, ATTN_IN_WIDTH), lambda i, j: (attn_layer, 0, 0)),
            pl.BlockSpec((1, Q_WIDTH, d), lambda i, j: (attn_layer, 0, 0)),
            pl.BlockSpec(memory_space=pltpu.SMEM),
        ] + later_specs,
        out_specs=out_specs,
        out_shape=out_shape,
        scratch_shapes=[
            pltpu.VMEM((Q_WIDTH, 2 * tm), jnp.bfloat16),
            pltpu.VMEM((Q_WIDTH, 2 * tm), jnp.bfloat16),
            pltpu.VMEM((BLOCK + step, N_KV_HEADS * LANES), jnp.bfloat16),
            pltpu.VMEM((N_KV_HEADS * V_ROWS, BLOCK + step), jnp.bfloat16),
            pltpu.VMEM((Q_WIDTH, 2 * tm), jnp.float32),
            pltpu.VMEM((Q_WIDTH, 2 * tm), jnp.bfloat16),
            pltpu.VMEM((ATTN_IN_WIDTH, d), jnp.bfloat16),
            pltpu.VMEM((Q_WIDTH, d), jnp.bfloat16),
        ],
        compiler_params=pltpu.CompilerParams(
            dimension_semantics=("arbitrary", "arbitrary"), vmem_limit_bytes=VMEM_LIMIT),
        name="attn_layer",
    )(x, mod_all, norm_g, positions, small, w_in_all, w_out_all, sinks, *later_args)


def _pool_kernel(*refs, n_later, layer, pool_layer):
    n_in = 7
    x_ref, mod_ref, g_ref, w_in_ref, w_grp_ref, scale_ref, w_out_ref = refs[:n_in]
    later = refs[n_in:n_in + (n_later + 2 if n_later else 0)]
    o_ref = refs[n_in + len(later)]
    v_s, w_in_s, w_grp_s, w_out_s = refs[-4:]
    tm = POOL_PART_TILE
    j = pl.program_id(1)
    if n_later:
        _later_adaln_step(later, refs[n_in + len(later) + 1], layer)
    shift, scale, gate = _modulation(mod_ref, pl.program_id(0))
    g_norm = g_ref[layer:layer + 1, :]
    mix_scale = scale_ref[pool_layer:pool_layer + 1, :]

    @pl.when((pl.program_id(0) == 0) & (j == 0))
    def _():
        for c in range(D_MODEL // LANES):
            rows = slice(c * LANES, (c + 1) * LANES)
            w_in_s[rows, :] = w_in_ref[0, rows, :].astype(jnp.bfloat16)
            w_out_s[rows, :] = w_out_ref[0, rows, :].astype(jnp.bfloat16)
        for gi in range(len(POOL_WINDOWS)):
            w_grp_s[gi] = w_grp_ref[0, gi].astype(jnp.bfloat16)

    @pl.when(j == 0)
    def _():
        v_s[0:POOL_HALO, :] = jnp.zeros((POOL_HALO, v_s.shape[1]), v_s.dtype)

    gates = {}

    def projection(hf):
        toks = slice(hf * tm, (hf + 1) * tm)
        st = {}

        def norm():
            st["h"] = _modulated_norm(x_ref[0, toks, :], shift, scale, g_norm)

        def value_path(cq):
            cols = slice(cq * POOL_PROJ_CHUNK, (cq + 1) * POOL_PROJ_CHUNK)
            v_s[POOL_HALO + hf * tm:POOL_HALO + (hf + 1) * tm, cols] = jnp.dot(
                st["h"], w_in_s[:, cols], preferred_element_type=jnp.float32)

        def gate_path(cq):
            cols = slice(D_MODEL + cq * POOL_PROJ_CHUNK, D_MODEL + (cq + 1) * POOL_PROJ_CHUNK)
            gates[hf, cq] = _silu(jnp.dot(st["h"], w_in_s[:, cols],
                                          preferred_element_type=jnp.float32))

        n_chunks = D_MODEL // POOL_PROJ_CHUNK
        return ([norm] + [(lambda cq=cq: value_path(cq)) for cq in range(n_chunks)]
                + [(lambda cq=cq: gate_path(cq)) for cq in range(n_chunks)])

    def mixing(hf):
        toks = slice(hf * tm, (hf + 1) * tm)
        ext = slice(hf * tm, POOL_HALO + (hf + 1) * tm)
        t = lax.broadcasted_iota(jnp.int32, (tm, POOL_GROUP_DIM), 0) + (j * POOL_PARTS + hf) * tm
        mixed = []

        def group(gi):
            w = POOL_WINDOWS[gi]
            cols = slice(gi * POOL_GROUP_DIM, (gi + 1) * POOL_GROUP_DIM)
            win = v_s[ext, cols]
            cur = win[POOL_HALO:]
            span = 1
            while span < w:
                win = win + pltpu.roll(win, span, 0)
                span *= 2
            count = jnp.minimum(t + 1, w).astype(jnp.float32)
            pooled = win[POOL_HALO:] / count - cur
            mixed.append(jnp.dot(pooled.astype(jnp.bfloat16), w_grp_s[gi],
                                 preferred_element_type=jnp.float32))

        def out():
            m = jnp.concatenate(mixed, axis=1) * mix_scale
            g_all = jnp.concatenate(
                [gates[hf, cq] for cq in range(D_MODEL // POOL_PROJ_CHUNK)], axis=1)
            o = jnp.dot((m * g_all).astype(jnp.bfloat16), w_out_s[...],
                        preferred_element_type=jnp.float32)
            o_ref[0, toks, :] = x_ref[0, toks, :] + gate * o

        return [(lambda gi=gi: group(gi)) for gi in range(len(POOL_WINDOWS))] + [out]

    def interleave(a, b):
        done_a = done_b = 0
        while done_a < len(a) or done_b < len(b):
            if done_b >= len(b) or (done_a < len(a) and done_a * len(b) <= done_b * len(a)):
                a[done_a]()
                done_a += 1
            else:
                b[done_b]()
                done_b += 1

    for f in projection(0):
        f()
    for part in range(1, POOL_PARTS):
        proj = projection(part)
        proj[0]()
        interleave(mixing(part - 1), proj[1:])
    for f in mixing(POOL_PARTS - 1):
        f()

    v_s[0:POOL_HALO, :] = v_s[POOL_PARTS * tm:POOL_PARTS * tm + POOL_HALO, :]


def _pool_layer(x, mod_all, mod_idx, norm_g, w_in_all, w_group_all, scale_all, w_out_all,
                layer, pool_layer, later_adaln=None):
    b, s, d = x.shape
    tm = POOL_PARTS * POOL_PART_TILE
    n_grp = len(POOL_WINDOWS)
    const = lambda i, j: (0, 0)
    later_specs, later_args, later_out_specs, later_out_shape, n_later = _later_adaln_operands(
        later_adaln, layer, b, d, s // tm)
    return pl.pallas_call(
        functools.partial(_pool_kernel, n_later=n_later, layer=layer, pool_layer=pool_layer),
        grid=(b, s // tm),
        in_specs=[
            pl.BlockSpec((1, tm, d), lambda i, j: (i, j, 0)),
            pl.BlockSpec((1, b, 3 * d), lambda i, j: (mod_idx, 0, 0)),
            pl.BlockSpec(norm_g.shape, const),
            pl.BlockSpec((1, d, 2 * D_MODEL), lambda i, j: (pool_layer, 0, 0)),
            pl.BlockSpec((1, n_grp, POOL_GROUP_DIM, POOL_GROUP_DIM), lambda i, j: (pool_layer, 0, 0, 0)),
            pl.BlockSpec(scale_all.shape, const),
            pl.BlockSpec((1, D_MODEL, d), lambda i, j: (pool_layer, 0, 0)),
        ] + later_specs,
        out_specs=[pl.BlockSpec((1, tm, d), lambda i, j: (i, j, 0))] + later_out_specs,
        out_shape=[jax.ShapeDtypeStruct(x.shape, x.dtype)] + later_out_shape,
        scratch_shapes=[
            pltpu.VMEM((POOL_HALO + tm, D_MODEL), jnp.float32),
            pltpu.VMEM((d, 2 * D_MODEL), jnp.bfloat16),
            pltpu.VMEM((n_grp, POOL_GROUP_DIM, POOL_GROUP_DIM), jnp.bfloat16),
            pltpu.VMEM((D_MODEL, d), jnp.bfloat16),
        ],
        compiler_params=pltpu.CompilerParams(
            dimension_semantics=("arbitrary", "arbitrary"), vmem_limit_bytes=VMEM_LIMIT),
        name="pool_layer",
    )(x, mod_all, norm_g, w_in_all, w_group_all, scale_all, w_out_all, *later_args)


def kernel(x, c, positions, ada_w, ada_b, norm_g, attn_w_in, attn_q_norm, attn_k_norm, attn_sinks,
           attn_w_out, pool_w_in, pool_w_group, pool_scale, pool_w_out):
    depth = ada_w.shape[0]
    small = _attn_small_params(attn_q_norm, attn_k_norm)
    n_front = min(ADALN_UPFRONT, depth)
    mod_front = _adaln(c, ada_w, ada_b, n_front)
    mods = [(mod_front, l) for l in range(n_front)]
    for i in range(depth):
        jl = i // 2
        mod_all, mod_idx = mods[i]
        later = (c, ada_w, ada_b) if (i == n_front - 1 and depth > n_front) else None
        if i % 2 == 0:
            outs = _attn_layer(x, mod_all, mod_idx, norm_g, positions, small, attn_w_in, attn_sinks,
                               attn_w_out, i, jl, later)
        else:
            outs = _pool_layer(x, mod_all, mod_idx, norm_g, pool_w_in, pool_w_group, pool_scale,
                               pool_w_out, i, jl, later)
        x = outs[0]
        if later is not None:
            mods += [(outs[1], l) for l in range(depth - n_front)]
    return x
```

```python
import functools
import math

import jax
import jax.numpy as jnp
from jax import lax
from jax.experimental import pallas as pl
from jax.experimental.pallas import tpu as pltpu

D_MODEL = 1024
HEAD_DIM = 64
N_HEADS = 16
N_KV_HEADS = 4
GROUP = N_HEADS // N_KV_HEADS
Q_WIDTH = N_HEADS * HEAD_DIM
KV_WIDTH = N_KV_HEADS * HEAD_DIM
ATTN_IN_WIDTH = 2 * Q_WIDTH + 2 * KV_WIDTH
BLOCK = 128
SUB = BLOCK // 2
ROT_DIM = HEAD_DIM // 4
ROT_HALF = ROT_DIM // 2
ROPE_THETA = 500000.0
POOL_WINDOWS = (2, 4, 8, 16)
POOL_GROUP_DIM = D_MODEL // len(POOL_WINDOWS)
POOL_HALO = 16
NORM_EPS = 1e-6
LANES = 128
BF16_ROWS = 16
V_PAD = BF16_ROWS
V_ROWS = HEAD_DIM + V_PAD
LOG2E = math.log2(math.e)

ATTN_HALF_TILE = 256
ATTN_PARTS = 8
PROJ_CHUNK = 512
POOL_PART_TILE = 512
POOL_PARTS = 4
POOL_PROJ_CHUNK = 256
ADALN_UPFRONT = 2
VMEM_LIMIT = 64 * 1024 * 1024

_NT = (((1,), (1,)), ((), ()))
_TN = (((0,), (0,)), ((), ()))


def _silu(x):
    hx = 0.5 * x
    return hx + hx * jnp.tanh(hx)


def _adaln_kernel(c_ref, w_ref, b_ref, o_ref):
    a = _silu(c_ref[...]).astype(jnp.bfloat16)
    w = w_ref[0].astype(jnp.bfloat16)
    bias = b_ref[pl.ds(pl.program_id(0), 1), :]
    o_ref[0] = jnp.dot(a, w, preferred_element_type=jnp.float32) + bias


def _adaln(c, ada_w, ada_b, depth):
    _, d, n = ada_w.shape
    b = c.shape[0]
    tn = 1024
    return pl.pallas_call(
        _adaln_kernel,
        grid=(depth, n // tn),
        in_specs=[
            pl.BlockSpec((b, d), lambda l, j: (0, 0)),
            pl.BlockSpec((1, d, tn), lambda l, j: (l, 0, j)),
            pl.BlockSpec((ada_b.shape[0], tn), lambda l, j: (0, j)),
        ],
        out_specs=pl.BlockSpec((1, b, tn), lambda l, j: (l, 0, j)),
        out_shape=jax.ShapeDtypeStruct((depth, b, n), jnp.float32),
        compiler_params=pltpu.CompilerParams(
            dimension_semantics=("arbitrary", "arbitrary"), vmem_limit_bytes=VMEM_LIMIT),
        name="adaln_mod",
    )(c, ada_w, ada_b)


def _modulation(mod_ref, batch):
    row = mod_ref[0, pl.ds(batch, 1), :]
    return row[:, :D_MODEL], row[:, D_MODEL:2 * D_MODEL], row[:, 2 * D_MODEL:]


def _modulated_norm(x, shift, scale, g):
    ms = jnp.mean(x * x, axis=-1, keepdims=True)
    gain = g * (1.0 + scale)
    h = x * lax.rsqrt(ms + NORM_EPS) * gain + shift
    return h.astype(jnp.bfloat16)


def _later_adaln_step(later, later_mod_ref, layer):
    c_ref, w_refs, b_ref = later[0], later[1:-1], later[-1]

    @pl.when((pl.program_id(0) == 0) & (pl.program_id(1) == 0))
    def _():
        for l in range(len(w_refs)):
            later_mod_ref[l] = jnp.broadcast_to(b_ref[layer + 1 + l:layer + 2 + l, :],
                                                later_mod_ref.shape[1:])

    a = _silu(c_ref[0]).astype(jnp.bfloat16)
    for l, w_ref in enumerate(w_refs):
        later_mod_ref[l] += jnp.dot(a, w_ref[0].astype(jnp.bfloat16),
                                    preferred_element_type=jnp.float32)


def _later_adaln_operands(later_adaln, layer, b, d, steps_per_seq):
    if later_adaln is None:
        return [], [], [], [], 0
    c, ada_w, ada_b = later_adaln
    first = layer + 1
    n_later = ada_w.shape[0] - first
    n_steps = b * steps_per_seq
    rows = d // n_steps
    assert rows * n_steps == d and rows % 8 == 0
    flat = lambda i, j: i * steps_per_seq + j
    specs = ([pl.BlockSpec((1, b, rows), lambda i, j: (flat(i, j), 0, 0))]
             + [pl.BlockSpec((1, rows, 3 * d), lambda i, j, l=l: (first + l, flat(i, j), 0))
                for l in range(n_later)]
             + [pl.BlockSpec(ada_b.shape, lambda i, j: (0, 0))])
    args = [c.reshape(b, n_steps, rows).transpose(1, 0, 2)] + [ada_w] * n_later + [ada_b]
    out_specs = [pl.BlockSpec((n_later, b, 3 * d), lambda i, j: (0, 0, 0))]
    out_shape = [jax.ShapeDtypeStruct((n_later, b, 3 * d), jnp.float32)]
    return specs, args, out_specs, out_shape, n_later


def _attn_kernel(*refs, n_later, layer, attn_layer):
    n_in = 8
    x_ref, mod_ref, g_ref, pos_ref, small_ref, w_in_ref, w_out_ref, sink_ref = refs[:n_in]
    later = refs[n_in:n_in + (n_later + 2 if n_later else 0)]
    refs = refs[n_in + len(later):]
    o_ref = refs[0]
    later_mod_ref = refs[1] if n_later else None
    qa_s, qb_s, k_s, vt_s, gt_s, ogt_s, w_in_t_s, w_out_s = refs[-8:]
    tm = ATTN_HALF_TILE
    batch = pl.program_id(0)
    j = pl.program_id(1)

    shift, scale, gate = _modulation(mod_ref, batch)
    g_norm = g_ref[layer:layer + 1, :]
    lane_tile = lambda a: jnp.concatenate([a] * (tm // LANES), axis=1)
    invf = lane_tile(small_ref[0:ROT_HALF, :])
    g0 = ROT_HALF + attn_layer * 2 * HEAD_DIM
    q_gain = lane_tile(small_ref[g0:g0 + HEAD_DIM, :])
    k_gain = lane_tile(small_ref[g0 + HEAD_DIM:g0 + 2 * HEAD_DIM, :])

    if n_later:
        _later_adaln_step(later, later_mod_ref, layer)

    @pl.when((pl.program_id(0) == 0) & (j == 0))
    def _():
        for c in range(ATTN_IN_WIDTH // LANES):
            cols = slice(c * LANES, (c + 1) * LANES)
            w_in_t_s[cols, :] = w_in_ref[0, :, cols].T.astype(jnp.bfloat16)
        for c in range(Q_WIDTH // LANES):
            rows = slice(c * LANES, (c + 1) * LANES)
            w_out_s[rows, :] = w_out_ref[0, rows, :].astype(jnp.bfloat16)

    @pl.when(j == 0)
    def _():
        k_s[0:BLOCK, :] = jnp.zeros((BLOCK, k_s.shape[1]), k_s.dtype)
        vt_s[:, 0:BLOCK] = jnp.zeros((vt_s.shape[0], BLOCK), vt_s.dtype)

    kq = lax.broadcasted_iota(jnp.int32, (SUB, 2 * BLOCK), 0)
    iq = lax.broadcasted_iota(jnp.int32, (SUB, 2 * BLOCK), 1) & (SUB - 1)
    use_new = kq <= iq
    prev_bias = jnp.where(j > 0, 0.0, -jnp.inf)
    lane4 = lax.broadcasted_iota(jnp.int32, (1, 2 * BLOCK), 1) // SUB
    low_q = (lax.broadcasted_iota(jnp.int32, (HEAD_DIM, tm), 1) & SUB) == 0
    low_o = lax.broadcasted_iota(jnp.int32, (HEAD_DIM, LANES), 1) < SUB
    zero = jnp.zeros((HEAD_DIM, LANES), jnp.bfloat16)
    zero_p = jnp.zeros((SUB, 2 * BLOCK), jnp.bfloat16)
    ones = jnp.ones((V_PAD, tm), jnp.bfloat16)

    def projection(hf):
        toks = slice(hf * tm, (hf + 1) * tm)
        ring = slice((hf % 2) * tm, (hf % 2 + 1) * tm)
        krows = slice(BLOCK + hf * tm, BLOCK + (hf + 1) * tm)
        st = {}

        def norm():
            st["h"] = _modulated_norm(x_ref[0, toks, :], shift, scale, g_norm)
            ang = pos_ref[pl.ds(batch, 1), toks].astype(jnp.float32) * invf
            st["cos"], st["sin"] = jnp.cos(ang), jnp.sin(ang)

        def norm_rope(t, gain):
            ssq = jnp.sum(t * t, axis=0, keepdims=True)
            tn = t * lax.rsqrt(ssq * (1.0 / HEAD_DIM) + NORM_EPS) * gain
            x1, x2 = tn[0:ROT_HALF], tn[ROT_HALF:ROT_DIM]
            cos, sin = st["cos"], st["sin"]
            return jnp.concatenate([x1 * cos - x2 * sin, x2 * cos + x1 * sin, tn[ROT_DIM:]], axis=0)

        def chunk(c):
            row0 = c * PROJ_CHUNK
            pt = lax.dot_general(w_in_t_s[row0:row0 + PROJ_CHUNK, :], st["h"], _NT,
                                 preferred_element_type=jnp.float32)
            if row0 >= Q_WIDTH + 2 * KV_WIDTH:
                g0 = row0 - (Q_WIDTH + 2 * KV_WIDTH)
                gt_s[g0:g0 + PROJ_CHUNK, ring] = _silu(pt)
                return
            for i in range(PROJ_CHUNK // HEAD_DIM):
                t = pt[i * HEAD_DIM:(i + 1) * HEAD_DIM]
                hd = row0 // HEAD_DIM + i
                if hd < N_HEADS:
                    qn = norm_rope(t, q_gain)
                    keep = low_q if hd % 2 == 0 else ~low_q
                    swapped = jnp.concatenate(
                        [pltpu.roll(qn[:, c2 * LANES:(c2 + 1) * LANES], SUB, 1)
                         for c2 in range(tm // LANES)], axis=1)
                    hrows = slice(hd * HEAD_DIM, (hd + 1) * HEAD_DIM)
                    qa_s[hrows, ring] = jnp.where(keep, qn, 0.0).astype(jnp.bfloat16)
                    qb_s[hrows, ring] = jnp.where(keep, swapped, 0.0).astype(jnp.bfloat16)
                elif hd < N_HEADS + N_KV_HEADS:
                    kh = hd - N_HEADS
                    kt = norm_rope(t, k_gain)
                    k_s[krows, kh * LANES:(kh + 1) * LANES] = (
                        jnp.concatenate([kt, kt], axis=0).T.astype(jnp.bfloat16))
                else:
                    kh = hd - N_HEADS - N_KV_HEADS
                    vt_s[kh * V_ROWS:kh * V_ROWS + HEAD_DIM, krows] = t.astype(jnp.bfloat16)
                    vt_s[kh * V_ROWS + HEAD_DIM:(kh + 1) * V_ROWS, krows] = ones

        def rank(c):
            row0 = c * PROJ_CHUNK
            return 1 if row0 < Q_WIDTH else (0 if row0 < Q_WIDTH + 2 * KV_WIDTH else 2)

        order = sorted(range(ATTN_IN_WIDTH // PROJ_CHUNK), key=rank)
        return [norm] + [(lambda c=c: chunk(c)) for c in order]

    def ring_cols(qb):
        rq = qb % (2 * tm // BLOCK)
        return slice(rq * BLOCK, (rq + 1) * BLOCK)

    def scores(qb):
        qcols = ring_cols(qb)
        out = []
        for kh in range(N_KV_HEADS):
            tiles = []
            for hq in range(2):
                u = 2 * qb + hq
                kd = k_s[u * SUB:(u + 3) * SUB, kh * LANES:(kh + 1) * LANES]
                blocks = []
                for g in range(GROUP):
                    hrows = slice((kh * GROUP + g) * HEAD_DIM, (kh * GROUP + g + 1) * HEAD_DIM)
                    src = (qa_s if g % 2 == hq else qb_s)[hrows, qcols]
                    blocks.append(jnp.concatenate([src, zero] if g < 2 else [zero, src], axis=1))
                sc = jnp.dot(jnp.concatenate([kd, kd], axis=1), jnp.concatenate(blocks, axis=0),
                             preferred_element_type=jnp.float32)
                old, mid, new = sc[:SUB], sc[SUB:2 * SUB], sc[2 * SUB:]
                if qb == 0:
                    old = old + prev_bias
                    if hq == 0:
                        mid = mid + prev_bias
                tiles.append(jnp.concatenate([mid, jnp.where(use_new, new, old)], axis=0))
            out.append(tiles)
        return out

    def finish(qb, tiles_all):
        qcols = ring_cols(qb)
        keys = slice(qb * BLOCK, (qb + 2) * BLOCK)
        for kh, tiles in enumerate(tiles_all):
            vt = vt_s[kh * V_ROWS:(kh + 1) * V_ROWS, keys]
            h0 = kh * GROUP
            sink = jnp.where(lane4 == 0, sink_ref[attn_layer, h0], jnp.where(
                lane4 == 1, sink_ref[attn_layer, h0 + 1], jnp.where(
                    lane4 == 2, sink_ref[attn_layer, h0 + 2],
                    sink_ref[attn_layer, h0 + 3]))) * LOG2E
            outs = []
            for hq, sc in enumerate(tiles):
                m = jnp.maximum(jnp.max(sc, axis=0, keepdims=True), sink)
                p = jnp.exp2(sc - m).astype(jnp.bfloat16)
                p_mid, p_fold = p[:SUB], p[SUB:]
                window = [jnp.where(use_new, zero_p, p_fold), p_mid, jnp.where(use_new, p_fold, zero_p)]
                p2 = jnp.concatenate(window + [zero_p] if hq == 0 else [zero_p] + window, axis=0)
                o = jnp.dot(vt, p2, preferred_element_type=jnp.float32)
                denom = o[HEAD_DIM:HEAD_DIM + 1] + jnp.exp2(sink - m)
                outs.append(o[:HEAD_DIM] * (1.0 / denom))
            for g in range(GROUP):
                par = g % 2
                cv = slice((g // 2) * LANES, (g // 2 + 1) * LANES)
                moved = pltpu.roll(outs[1 - par][:, cv], SUB, 1)
                full = jnp.where(low_o if par == 0 else ~low_o, outs[par][:, cv], moved)
                hrows = slice((h0 + g) * HEAD_DIM, (h0 + g + 1) * HEAD_DIM)
                ogt_s[hrows, qcols] = (full * gt_s[hrows, qcols]).astype(jnp.bfloat16)

    def attention(hf):
        blocks = range(hf * (tm // BLOCK), (hf + 1) * (tm // BLOCK))
        st = {}

        def score_step(qb):
            st[qb] = scores(qb)

        def finish_step(qb):
            finish(qb, st.pop(qb))

        steps = [lambda: score_step(blocks[0])]
        for qb in blocks:
            if qb + 1 in blocks:
                steps.append(lambda qb=qb: score_step(qb + 1))
            steps.append(lambda qb=qb: finish_step(qb))
        return steps

    def out_projection(hf):
        toks = slice(hf * tm, (hf + 1) * tm)
        ring = slice((hf % 2) * tm, (hf % 2 + 1) * tm)

        def run():
            out = lax.dot_general(ogt_s[:, ring], w_out_s[...], _TN,
                                  preferred_element_type=jnp.float32)
            o_ref[0, toks, :] = x_ref[0, toks, :] + gate * out

        return [run]

    def interleave(a, b):
        done_a = done_b = 0
        while done_a < len(a) or done_b < len(b):
            if done_b >= len(b) or (done_a < len(a) and done_a * len(b) <= done_b * len(a)):
                a[done_a]()
                done_a += 1
            else:
                b[done_b]()
                done_b += 1

    for f in projection(0):
        f()
    for part in range(1, ATTN_PARTS):
        proj = projection(part)
        proj[0]()
        interleave(attention(part - 1) + out_projection(part - 1), proj[1:])
    for f in attention(ATTN_PARTS - 1) + out_projection(ATTN_PARTS - 1):
        f()

    step_tokens = ATTN_PARTS * tm
    k_s[0:BLOCK, :] = k_s[step_tokens:step_tokens + BLOCK, :]
    vt_s[:, 0:BLOCK] = vt_s[:, step_tokens:step_tokens + BLOCK]


def _attn_small_params(q_norm, k_norm):
    inv_freq = ROPE_THETA ** (-jnp.arange(ROT_HALF, dtype=jnp.float32) * 2.0 / ROT_DIM)
    cols = [inv_freq]
    for l in range(q_norm.shape[0]):
        cols += [q_norm[l] * (HEAD_DIM ** -0.5 * LOG2E), k_norm[l]]
    col = jnp.concatenate(cols)
    return jnp.broadcast_to(col[:, None], (col.shape[0], LANES))


def _attn_layer(x, mod_all, mod_idx, norm_g, positions, small, w_in_all, sinks, w_out_all,
                layer, attn_layer, later_adaln=None):
    b, s, d = x.shape
    tm = ATTN_HALF_TILE
    step = ATTN_PARTS * tm
    steps_per_seq = s // step
    const = lambda i, j: (0, 0)
    later_specs, later_args, later_out_specs, later_out_shape, n_later = _later_adaln_operands(
        later_adaln, layer, b, d, steps_per_seq)
    out_specs = [pl.BlockSpec((1, step, d), lambda i, j: (i, j, 0))] + later_out_specs
    out_shape = [jax.ShapeDtypeStruct(x.shape, x.dtype)] + later_out_shape
    return pl.pallas_call(
        functools.partial(_attn_kernel, n_later=n_later, layer=layer, attn_layer=attn_layer),
        grid=(b, steps_per_seq),
        in_specs=[
            pl.BlockSpec((1, step, d), lambda i, j: (i, j, 0)),
            pl.BlockSpec((1, b, 3 * d), lambda i, j: (mod_idx, 0, 0)),
            pl.BlockSpec(norm_g.shape, const),
            pl.BlockSpec((b, step), lambda i, j: (0, j)),
            pl.BlockSpec(small.shape, const),
            pl.BlockSpec((1, d, ATTN_IN_WIDTH), lambda i, j: (attn_layer, 0, 0)),
            pl.BlockSpec((1, Q_WIDTH, d), lambda i, j: (attn_layer, 0, 0)),
            pl.BlockSpec(memory_space=pltpu.SMEM),
        ] + later_specs,
        out_specs=out_specs,
        out_shape=out_shape,
        scratch_shapes=[
            pltpu.VMEM((Q_WIDTH, 2 * tm), jnp.bfloat16),
            pltpu.VMEM((Q_WIDTH, 2 * tm), jnp.bfloat16),
            pltpu.VMEM((BLOCK + step, N_KV_HEADS * LANES), jnp.bfloat16),
            pltpu.VMEM((N_KV_HEADS * V_ROWS, BLOCK + step), jnp.bfloat16),
            pltpu.VMEM((Q_WIDTH, 2 * tm), jnp.float32),
            pltpu.VMEM((Q_WIDTH, 2 * tm), jnp.bfloat16),
            pltpu.VMEM((ATTN_IN_WIDTH, d), jnp.bfloat16),
            pltpu.VMEM((Q_WIDTH, d), jnp.bfloat16),
        ],
        compiler_params=pltpu.CompilerParams(
            dimension_semantics=("arbitrary", "arbitrary"), vmem_limit_bytes=VMEM_LIMIT),
        name="attn_layer",
    )(x, mod_all, norm_g, positions, small, w_in_all, w_out_all, sinks, *later_args)


def _pool_kernel(*refs, n_later, layer, pool_layer):
    n_in = 7
    x_ref, mod_ref, g_ref, w_in_ref, w_grp_ref, scale_ref, w_out_ref = refs[:n_in]
    later = refs[n_in:n_in + (n_later + 2 if n_later else 0)]
    o_ref = refs[n_in + len(later)]
    v_s, w_in_s, w_grp_s, w_out_s, w_sem = refs[-5:]
    tm = POOL_PART_TILE
    j = pl.program_id(1)
    if n_later:
        _later_adaln_step(later, refs[n_in + len(later) + 1], layer)
    shift, scale, gate = _modulation(mod_ref, pl.program_id(0))
    g_norm = g_ref[layer:layer + 1, :]
    mix_scale = scale_ref[pool_layer:pool_layer + 1, :]

    @pl.when((pl.program_id(0) == 0) & (j == 0))
    def _():
        stage = [slice(POOL_HALO + k * D_MODEL, POOL_HALO + (k + 1) * D_MODEL) for k in range(2)]
        halves = [slice(k * D_MODEL, (k + 1) * D_MODEL) for k in range(2)]
        copies = [pltpu.make_async_copy(w_in_ref.at[pool_layer, :, halves[k]], v_s.at[stage[k], :],
                                        w_sem.at[k]) for k in range(2)]
        out_copy = pltpu.make_async_copy(w_out_ref.at[pool_layer], v_s.at[stage[0], :], w_sem.at[2])
        for cp in copies:
            cp.start()
        for k in range(2):
            copies[k].wait()
            for c in range(D_MODEL // LANES):
                rows = slice(stage[k].start + c * LANES, stage[k].start + (c + 1) * LANES)
                w_in_s[c * LANES:(c + 1) * LANES, halves[k]] = v_s[rows, :].astype(jnp.bfloat16)
            if k == 0:
                out_copy.start()
        out_copy.wait()
        for c in range(D_MODEL // LANES):
            rows = slice(stage[0].start + c * LANES, stage[0].start + (c + 1) * LANES)
            w_out_s[c * LANES:(c + 1) * LANES, :] = v_s[rows, :].astype(jnp.bfloat16)
        for gi in range(len(POOL_WINDOWS)):
            w_grp_s[gi] = w_grp_ref[0, gi].astype(jnp.bfloat16)

    @pl.when(j == 0)
    def _():
        v_s[0:POOL_HALO, :] = jnp.zeros((POOL_HALO, v_s.shape[1]), v_s.dtype)

    gates = {}

    def projection(hf):
        toks = slice(hf * tm, (hf + 1) * tm)
        st = {}

        def norm():
            st["h"] = _modulated_norm(x_ref[0, toks, :], shift, scale, g_norm)

        def value_path(cq):
            cols = slice(cq * POOL_PROJ_CHUNK, (cq + 1) * POOL_PROJ_CHUNK)
            v_s[POOL_HALO + hf * tm:POOL_HALO + (hf + 1) * tm, cols] = jnp.dot(
                st["h"], w_in_s[:, cols], preferred_element_type=jnp.float32)

        def gate_path(cq):
            cols = slice(D_MODEL + cq * POOL_PROJ_CHUNK, D_MODEL + (cq + 1) * POOL_PROJ_CHUNK)
            gates[hf, cq] = _silu(jnp.dot(st["h"], w_in_s[:, cols],
                                          preferred_element_type=jnp.float32))

        n_chunks = D_MODEL // POOL_PROJ_CHUNK
        return ([norm] + [(lambda cq=cq: value_path(cq)) for cq in range(n_chunks)]
                + [(lambda cq=cq: gate_path(cq)) for cq in range(n_chunks)])

    def mixing(hf):
        toks = slice(hf * tm, (hf + 1) * tm)
        ext = slice(hf * tm, POOL_HALO + (hf + 1) * tm)
        t = lax.broadcasted_iota(jnp.int32, (tm, POOL_GROUP_DIM), 0) + (j * POOL_PARTS + hf) * tm
        mixed = []

        def group(gi):
            w = POOL_WINDOWS[gi]
            cols = slice(gi * POOL_GROUP_DIM, (gi + 1) * POOL_GROUP_DIM)
            win = v_s[ext, cols]
            cur = win[POOL_HALO:]
            span = 1
            while span < w:
                win = win + pltpu.roll(win, span, 0)
                span *= 2
            count = jnp.minimum(t + 1, w).astype(jnp.float32)
            pooled = win[POOL_HALO:] / count - cur
            mixed.append(jnp.dot(pooled.astype(jnp.bfloat16), w_grp_s[gi],
                                 preferred_element_type=jnp.float32))

        def out():
            m = jnp.concatenate(mixed, axis=1) * mix_scale
            g_all = jnp.concatenate(
                [gates[hf, cq] for cq in range(D_MODEL // POOL_PROJ_CHUNK)], axis=1)
            o = jnp.dot((m * g_all).astype(jnp.bfloat16), w_out_s[...],
                        preferred_element_type=jnp.float32)
            o_ref[0, toks, :] = x_ref[0, toks, :] + gate * o

        return [(lambda gi=gi: group(gi)) for gi in range(len(POOL_WINDOWS))] + [out]

    def interleave(a, b):
        done_a = done_b = 0
        while done_a < len(a) or done_b < len(b):
            if done_b >= len(b) or (done_a < len(a) and done_a * len(b) <= done_b * len(a)):
                a[done_a]()
                done_a += 1
            else:
                b[done_b]()
                done_b += 1

    for f in projection(0):
        f()
    for part in range(1, POOL_PARTS):
        proj = projection(part)
        proj[0]()
        interleave(mixing(part - 1), proj[1:])
    for f in mixing(POOL_PARTS - 1):
        f()

    v_s[0:POOL_HALO, :] = v_s[POOL_PARTS * tm:POOL_PARTS * tm + POOL_HALO, :]


def _pool_layer(x, mod_all, mod_idx, norm_g, w_in_all, w_group_all, scale_all, w_out_all,
                layer, pool_layer, later_adaln=None):
    b, s, d = x.shape
    tm = POOL_PARTS * POOL_PART_TILE
    n_grp = len(POOL_WINDOWS)
    const = lambda i, j: (0, 0)
    later_specs, later_args, later_out_specs, later_out_shape, n_later = _later_adaln_operands(
        later_adaln, layer, b, d, s // tm)
    return pl.pallas_call(
        functools.partial(_pool_kernel, n_later=n_later, layer=layer, pool_layer=pool_layer),
        grid=(b, s // tm),
        in_specs=[
            pl.BlockSpec((1, tm, d), lambda i, j: (i, j, 0)),
            pl.BlockSpec((1, b, 3 * d), lambda i, j: (mod_idx, 0, 0)),
            pl.BlockSpec(norm_g.shape, const),
            pl.BlockSpec(memory_space=pl.ANY),
            pl.BlockSpec((1, n_grp, POOL_GROUP_DIM, POOL_GROUP_DIM), lambda i, j: (pool_layer, 0, 0, 0)),
            pl.BlockSpec(scale_all.shape, const),
            pl.BlockSpec(memory_space=pl.ANY),
        ] + later_specs,
        out_specs=[pl.BlockSpec((1, tm, d), lambda i, j: (i, j, 0))] + later_out_specs,
        out_shape=[jax.ShapeDtypeStruct(x.shape, x.dtype)] + later_out_shape,
        scratch_shapes=[
            pltpu.VMEM((POOL_HALO + tm, D_MODEL), jnp.float32),
            pltpu.VMEM((d, 2 * D_MODEL), jnp.bfloat16),
            pltpu.VMEM((n_grp, POOL_GROUP_DIM, POOL_GROUP_DIM), jnp.bfloat16),
            pltpu.VMEM((D_MODEL, d), jnp.bfloat16),
            pltpu.SemaphoreType.DMA((3,)),
        ],
        compiler_params=pltpu.CompilerParams(
            dimension_semantics=("arbitrary", "arbitrary"), vmem_limit_bytes=VMEM_LIMIT),
        name="pool_layer",
    )(x, mod_all, norm_g, w_in_all, w_group_all, scale_all, w_out_all, *later_args)


def kernel(x, c, positions, ada_w, ada_b, norm_g, attn_w_in, attn_q_norm, attn_k_norm, attn_sinks,
           attn_w_out, pool_w_in, pool_w_group, pool_scale, pool_w_out):
    depth = ada_w.shape[0]
    small = _attn_small_params(attn_q_norm, attn_k_norm)
    n_front = min(ADALN_UPFRONT, depth)
    mod_front = _adaln(c, ada_w, ada_b, n_front)
    mods = [(mod_front, l) for l in range(n_front)]
    for i in range(depth):
        jl = i // 2
        mod_all, mod_idx = mods[i]
        later = (c, ada_w, ada_b) if (i == n_front - 1 and depth > n_front) else None
        if i % 2 == 0:
            outs = _attn_layer(x, mod_all, mod_idx, norm_g, positions, small, attn_w_in, attn_sinks,
                               attn_w_out, i, jl, later)
        else:
            outs = _pool_layer(x, mod_all, mod_idx, norm_g, pool_w_in, pool_w_group, pool_scale,
                               pool_w_out, i, jl, later)
        x = outs[0]
        if later is not None:
            mods += [(outs[1], l) for l in range(depth - n_front)]
    return x
```
